```python
import jax, jax.numpy as jnp
from jax import lax
import numpy as np

D_MODEL = 2048
BATCH = 4
SEQ = 2048
DEPTH = 4
DEC_BATCH = 8
DEC_SEQ = 1
PAST_LEN = 16384
PAGE_SIZE = 128

D_MIX = D_MODEL
C_CONV = D_MIX // 2
N_HEADS = 16
HEAD_DIM = (D_MIX - C_CONV) // N_HEADS
N_KV = 4
GROUP = N_HEADS // N_KV
HD = N_HEADS * HEAD_DIM
KVD = N_KV * HEAD_DIM
CONV_W = 31
CMP_BLOCK = 64
SEL_BLOCK = 64
TOP_N = 16
WINDOW = 512
CMP_HIDDEN = 2 * HEAD_DIM
ROPE_DIM = HEAD_DIM // 4
ROPE_THETA = 500000.0
Q_BLOCK = 64
EPS = 1e-6
BIG = 1e9
NEG = -1e30
N_IN = 3 * C_CONV + 2 * HD + 6 * KVD + 3 * N_HEADS

kernel_name = "hymba_conformer_nsa_decode_step"


def _rmsnorm(x, g):
    xf = x.astype(jnp.float32)
    y = xf * lax.rsqrt(jnp.mean(xf * xf, axis=-1, keepdims=True) + EPS)
    return y.astype(x.dtype) * g


def _layernorm(x, g, b):
    xf = x.astype(jnp.float32)
    mu = jnp.mean(xf, axis=-1, keepdims=True)
    var = jnp.mean(jnp.square(xf - mu), axis=-1, keepdims=True)
    return ((xf - mu) * lax.rsqrt(var + EPS)).astype(x.dtype) * g + b


def _rope(x, pos):
    half = ROPE_DIM // 2
    inv = ROPE_THETA ** (-jnp.arange(half, dtype=jnp.float32) / half)
    ang = pos.astype(jnp.float32)[:, None] * inv[None, :]
    cos = jnp.cos(ang)[None, :, None, :]
    sin = jnp.sin(ang)[None, :, None, :]
    xr = x[..., :ROPE_DIM].astype(jnp.float32)
    x1, x2 = xr[..., :half], xr[..., half:]
    rot = jnp.concatenate([x1 * cos - x2 * sin, x2 * cos + x1 * sin], axis=-1)
    return jnp.concatenate([rot.astype(x.dtype), x[..., ROPE_DIM:]], axis=-1)


def _masked_softmax(s, mask):
    s = jnp.where(mask, s.astype(jnp.float32), NEG)
    m = jnp.max(s, axis=-1, keepdims=True)
    e = jnp.where(mask, jnp.exp(s - m), 0.0)
    return e / jnp.maximum(jnp.sum(e, axis=-1, keepdims=True), 1e-30)


def _nsa(q, q_rot, gates, kv_all, win_ext, q0, cmp_pos, cmp_w1, cmp_b1, cmp_w2, cmp_b2):
    B, T = q.shape[0], q.shape[1]
    L = kv_all.shape[1]
    scale = HEAD_DIM ** -0.5
    nbc = L // CMP_BLOCK
    blocks = kv_all[:, :nbc * CMP_BLOCK, :2].reshape(B, nbc, CMP_BLOCK, 2, N_KV, HEAD_DIM)
    blocks = blocks + cmp_pos.transpose(1, 0, 2)[None, None, :, :, None, :]
    flat = blocks.transpose(0, 1, 3, 4, 2, 5).reshape(B, nbc, 2, N_KV, CMP_BLOCK * HEAD_DIM)
    hid = jax.nn.silu(jnp.einsum('bnckf,cfe->bncke', flat, cmp_w1) + cmp_b1[:, None, :])
    comp = jnp.einsum('bncke,ced->bnckd', hid, cmp_w2) + cmp_b2[:, None, :]
    k_cmp, v_cmp = comp[:, :, 0], comp[:, :, 1]
    nbs = -(-L // SEL_BLOCK)
    sel = jnp.pad(kv_all[:, :, 2:], ((0, 0), (0, nbs * SEL_BLOCK - L), (0, 0), (0, 0), (0, 0)))
    sel = sel.reshape(B, nbs, SEL_BLOCK, 2, N_KV, HEAD_DIM).transpose(3, 0, 4, 1, 2, 5)
    k_sel_b, v_sel_b = sel[0], sel[1]
    k_top = min(TOP_N, nbs)
    qb = Q_BLOCK if T % Q_BLOCK == 0 else T
    nqb = T // qb
    qg = q.reshape(B, T, N_KV, GROUP, HEAD_DIM)
    qrg = q_rot.reshape(B, T, N_KV, GROUP, HEAD_DIM)
    gg = gates.reshape(B, T, N_KV, GROUP, 3)
    blk_ids = jnp.arange(nbs, dtype=jnp.int32)
    cmp_end = (jnp.arange(nbc, dtype=jnp.int32) + 1) * CMP_BLOCK - 1
    gather = jax.vmap(jax.vmap(lambda kb, i: kb[i]))

    def one_block(i):
        qs = i * qb
        qc = lax.dynamic_slice_in_dim(qg, qs, qb, axis=1)
        qr = lax.dynamic_slice_in_dim(qrg, qs, qb, axis=1)
        g = lax.dynamic_slice_in_dim(gg, qs, qb, axis=1)
        tpos = q0 + qs + jnp.arange(qb, dtype=jnp.int32)
        s_c = jnp.einsum('bqkgd,bnkd->bkgqn', qc, k_cmp) * scale
        p_c = _masked_softmax(s_c, cmp_end[None, :] <= tpos[:, None])
        o_c = jnp.einsum('bkgqn,bnkd->bqkgd', p_c.astype(v_cmp.dtype), v_cmp)
        imp = jnp.pad(jnp.sum(p_c, axis=2), ((0, 0), (0, 0), (0, 0), (0, nbs - nbc)))
        cur = (tpos // SEL_BLOCK)[:, None]
        forced = (blk_ids == 0) | (blk_ids == cur) | (blk_ids == cur - 1)
        score = jnp.where(blk_ids <= cur, jnp.where(forced, BIG, imp), -BIG)
        top_val, top_idx = lax.top_k(score, k_top)
        k_sel = gather(k_sel_b, top_idx).reshape(B, N_KV, qb, k_top * SEL_BLOCK, HEAD_DIM)
        v_sel = gather(v_sel_b, top_idx).reshape(B, N_KV, qb, k_top * SEL_BLOCK, HEAD_DIM)
        kpos = top_idx[..., None] * SEL_BLOCK + jnp.arange(SEL_BLOCK, dtype=jnp.int32)
        m_s = (top_val[..., None] > -0.5 * BIG) & (kpos <= tpos[None, None, :, None, None])
        m_s = m_s.reshape(B, N_KV, qb, k_top * SEL_BLOCK)
        s_s = jnp.einsum('bqkgd,bkqmd->bkgqm', qr, k_sel) * scale
        p_s = _masked_softmax(s_s, m_s[:, :, None])
        o_s = jnp.einsum('bkgqm,bkqmd->bqkgd', p_s.astype(v_sel.dtype), v_sel)
        wk = lax.dynamic_slice_in_dim(win_ext, qs, WINDOW + qb, axis=1)
        wpos = q0 - WINDOW + qs + jnp.arange(WINDOW + qb, dtype=jnp.int32)
        dlt = tpos[:, None] - wpos[None, :]
        m_w = (dlt >= 0) & (dlt < WINDOW) & (wpos[None, :] >= 0)
        s_w = jnp.einsum('bqkgd,bmkd->bkgqm', qr, wk[:, :, 0]) * scale
        p_w = _masked_softmax(s_w, m_w)
        o_w = jnp.einsum('bkgqm,bmkd->bqkgd', p_w.astype(wk.dtype), wk[:, :, 1])
        return g[..., 0:1] * o_c + g[..., 1:2] * o_s + g[..., 2:3] * o_w

    out = lax.map(one_block, jnp.arange(nqb, dtype=jnp.int32))
    return out.transpose(1, 0, 2, 3, 4, 5).reshape(B, T, HD)


def _layer(x, q0, kv_past, win_buf, conv_buf, n_keep, w_in, w_out, g_pre, g_post,
           conv_w, conv_b, ln_g, ln_b, cmp_pos, cmp_w1, cmp_b1, cmp_w2, cmp_b2, gate_b):
    B, T, _ = x.shape
    pos = q0 + jnp.arange(T, dtype=jnp.int32)
    u = _rmsnorm(x, g_pre)
    proj = u @ w_in
    cuts = [C_CONV, 2 * C_CONV, 3 * C_CONV, 3 * C_CONV + HD, 3 * C_CONV + 2 * HD,
            3 * C_CONV + 2 * HD + 6 * KVD]
    a_val, a_gate, z_conv, q, z_attn, kv6, gl = jnp.split(proj, cuts, axis=-1)
    glu = a_val * jax.nn.sigmoid(a_gate)
    ap = jnp.concatenate([conv_buf, glu], axis=1)
    c = lax.conv_general_dilated(ap, conv_w[:, None, :], (1,), 'VALID',
                                 dimension_numbers=('NWC', 'WIO', 'NWC'),
                                 feature_group_count=C_CONV) + conv_b
    y_conv = jax.nn.silu(_layernorm(c, ln_g, ln_b)) * jax.nn.silu(z_conv)
    conv_new = ap[:, ap.shape[1] - (CONV_W - 1):]
    q = q.reshape(B, T, N_HEADS, HEAD_DIM)
    kc, vc, ks, vs, kw, vw = [t.reshape(B, T, N_KV, HEAD_DIM) for t in jnp.split(kv6, 6, axis=-1)]
    q_rot = _rope(q, pos)
    ks = _rope(ks, pos)
    kw = _rope(kw, pos)
    gates = jax.nn.sigmoid(gl + gate_b).reshape(B, T, N_HEADS, 3)
    kv_new = jnp.stack([kc, vc, ks, vs], axis=2)
    kv_all = jnp.concatenate([kv_past, kv_new], axis=1)
    win_rows = jnp.stack([kw, vw], axis=2)
    pad_rows = jnp.zeros((B, WINDOW - win_buf.shape[1], 2, N_KV, HEAD_DIM), x.dtype)
    win_ext = jnp.concatenate([pad_rows, win_buf, win_rows], axis=1)
    win_new = win_ext[:, win_ext.shape[1] - n_keep:]
    o = _nsa(q, q_rot, gates, kv_all, win_ext, q0, cmp_pos, cmp_w1, cmp_b1, cmp_w2, cmp_b2)
    mix = jnp.concatenate([y_conv, o * jax.nn.silu(z_attn)], axis=-1)
    y = x + _rmsnorm(mix @ w_out, g_post)
    return y, kv_new, win_new, conv_new


def setup_inputs(seed: int = 0) -> dict:
    key = jax.random.key(seed)
    ks = jax.random.split(key, 20)
    n_pages = PAST_LEN // PAGE_SIZE
    n_used = DEC_BATCH * n_pages
    n_pool = n_used + max(1, n_used // 4)
    w_buf = min(WINDOW, PAST_LEN)
    f = jnp.float32
    nrm = lambda k, s, sc: jax.random.normal(k, s, f) * sc
    page_table = jax.random.permutation(ks[5], n_pool)[:n_used].reshape(DEC_BATCH, n_pages).astype(jnp.int32)
    return {
        "x_prompt": nrm(ks[0], (BATCH, SEQ, D_MODEL), 1.0),
        "x_sample": nrm(ks[1], (DEC_BATCH, DEC_SEQ, D_MODEL), 1.0),
        "cache_kv_pages": nrm(ks[2], (DEPTH, n_pool, PAGE_SIZE, 4, N_KV, HEAD_DIM), 1.0),
        "cache_win": nrm(ks[3], (DEPTH, DEC_BATCH, w_buf, 2, N_KV, HEAD_DIM), 1.0),
        "state_conv": nrm(ks[4], (DEPTH, DEC_BATCH, CONV_W - 1, C_CONV), 0.5),
        "page_table": page_table,
        "w_in": nrm(ks[6], (DEPTH, D_MODEL, N_IN), D_MODEL ** -0.5),
        "w_out": nrm(ks[7], (DEPTH, D_MIX, D_MODEL), D_MIX ** -0.5),
        "norm_pre": 1.0 + nrm(ks[8], (DEPTH, D_MODEL), 0.05),
        "norm_post": 1.0 + nrm(ks[9], (DEPTH, D_MODEL), 0.05),
        "conv_w": nrm(ks[10], (DEPTH, CONV_W, C_CONV), CONV_W ** -0.5),
        "conv_b": nrm(ks[11], (DEPTH, C_CONV), 0.02),
        "conv_ln_g": 1.0 + nrm(ks[12], (DEPTH, C_CONV), 0.05),
        "conv_ln_b": nrm(ks[13], (DEPTH, C_CONV), 0.02),
        "cmp_pos": nrm(ks[14], (DEPTH, 2, CMP_BLOCK, HEAD_DIM), 0.1),
        "cmp_w1": nrm(ks[15], (DEPTH, 2, CMP_BLOCK * HEAD_DIM, CMP_HIDDEN), (CMP_BLOCK * HEAD_DIM) ** -0.5),
        "cmp_b1": nrm(ks[16], (DEPTH, 2, CMP_HIDDEN), 0.02),
        "cmp_w2": nrm(ks[17], (DEPTH, 2, CMP_HIDDEN, HEAD_DIM), CMP_HIDDEN ** -0.5),
        "cmp_b2": nrm(ks[18], (DEPTH, 2, HEAD_DIM), 0.02),
        "gate_b": nrm(ks[19], (DEPTH, 3 * N_HEADS), 0.1),
    }


def reference(x_prompt, x_sample, cache_kv_pages, cache_win, state_conv, page_table,
              w_in, w_out, norm_pre, norm_post, conv_w, conv_b, conv_ln_g, conv_ln_b,
              cmp_pos, cmp_w1, cmp_b1, cmp_w2, cmp_b2, gate_b):
    B, S = x_prompt.shape[0], x_prompt.shape[1]
    DB = x_sample.shape[0]
    past_len = page_table.shape[1] * cache_kv_pages.shape[2]
    dt = x_prompt.dtype
    yp, ys = x_prompt, x_sample
    kvp_l, kvs_l, winp_l, wins_l, convp_l, convs_l = [], [], [], [], [], []
    for l in range(DEPTH):
        w = (w_in[l], w_out[l], norm_pre[l], norm_post[l], conv_w[l], conv_b[l],
             conv_ln_g[l], conv_ln_b[l], cmp_pos[l], cmp_w1[l], cmp_b1[l], cmp_w2[l],
             cmp_b2[l], gate_b[l])
        kv0 = jnp.zeros((B, 0, 4, N_KV, HEAD_DIM), dt)
        win0 = jnp.zeros((B, 0, 2, N_KV, HEAD_DIM), dt)
        conv0 = jnp.zeros((B, CONV_W - 1, C_CONV), dt)
        yp, kvp, winp, convp = _layer(yp, 0, kv0, win0, conv0, min(WINDOW, S), *w)
        kv_past = cache_kv_pages[l][page_table].reshape(DB, past_len, 4, N_KV, HEAD_DIM)
        ys, kvs, wins, convs = _layer(ys, past_len, kv_past, cache_win[l], state_conv[l],
                                      cache_win.shape[2], *w)
        kvp_l.append(kvp); kvs_l.append(kvs); winp_l.append(winp)
        wins_l.append(wins); convp_l.append(convp); convs_l.append(convs)
    kv_prompt = jnp.stack(kvp_l)
    kv_sample = jnp.stack(kvs_l)
    win_prompt = jnp.stack(winp_l)
    win_sample = jnp.stack(wins_l)
    conv_prompt = jnp.stack(convp_l)
    conv_sample = jnp.stack(convs_l)
    return (yp, ys, kv_prompt, kv_sample, win_prompt, win_sample, conv_prompt, conv_sample)
```

```python
import functools

import jax
import jax.numpy as jnp
from jax import lax
from jax.experimental import pallas as pl
from jax.experimental.pallas import tpu as pltpu

F32 = jnp.float32
BF16 = jnp.bfloat16

TOP_N = 16
WINDOW = 512
ROPE_THETA = 500000.0
EPS = 1e-6
BIG = 1e9
NEG = -1e30
LANES = 128
VMEM_LIMIT = 56 * 1024 * 1024

NT_DIMS = (((1,), (1,)), ((), ()))


def _silu(x):
    return x * jax.nn.sigmoid(x)


def _cparams(sem):
    return pltpu.CompilerParams(dimension_semantics=sem, vmem_limit_bytes=VMEM_LIMIT)


def _inproj_body(x_ref, g_ref, w_ref, wg_ref, o_ref, gl_ref, u_ref):
    @pl.when(pl.program_id(1) == 0)
    def _():
        x = x_ref[...]
        ms = jnp.mean(x * x, axis=-1, keepdims=True)
        u = (x * lax.rsqrt(ms + EPS)) * g_ref[...]
        u_ref[...] = u.astype(u_ref.dtype)
        gl_ref[...] = jnp.dot(u.astype(BF16), wg_ref[...], preferred_element_type=F32)

    o_ref[...] = jnp.dot(u_ref[...].astype(BF16), w_ref[...], preferred_element_type=F32)


def _inproj(x, g, w_main, w_gate, l, *, tm, tn):
    M, D = x.shape
    n_main = w_main.shape[-1]
    u_dtype = BF16 if tm % 16 == 0 else F32
    return pl.pallas_call(
        _inproj_body,
        grid=(M // tm, n_main // tn),
        in_specs=[
            pl.BlockSpec((tm, D), lambda i, j: (i, 0)),
            pl.BlockSpec((None, 1, D), lambda i, j: (l, 0, 0)),
            pl.BlockSpec((None, D, tn), lambda i, j: (l, 0, j)),
            pl.BlockSpec((None, D, LANES), lambda i, j: (l, 0, 0)),
        ],
        out_specs=[
            pl.BlockSpec((tm, tn), lambda i, j: (i, j)),
            pl.BlockSpec((tm, LANES), lambda i, j: (i, 0)),
        ],
        out_shape=[jax.ShapeDtypeStruct((M, n_main), F32),
                   jax.ShapeDtypeStruct((M, LANES), F32)],
        scratch_shapes=[pltpu.VMEM((tm, D), u_dtype)],
        compiler_params=_cparams(("parallel", "arbitrary")),
        name="inproj",
    )(x, g, w_main, w_gate)


def _conv_body(av_ref, ag_ref, z_ref, buf_ref, w_ref, cb_ref, lg_ref, lb_ref,
               y_ref, cn_ref, ap_ref, *, tt, nt, taps, rc):
    t = pl.program_id(1)
    hist = -(-(taps - 1) // 8) * 8
    off = hist - (taps - 1)
    C = ap_ref.shape[1]

    @pl.when(t == 0)
    def _():
        ap_ref[0:off, :] = jnp.zeros((off, C), F32)
        ap_ref[off:hist, :] = buf_ref[...]

    @pl.when(t > 0)
    def _():
        ap_ref[0:hist, :] = ap_ref[tt:tt + hist, :]

    ap_ref[hist:hist + tt, :] = av_ref[...] * jax.nn.sigmoid(ag_ref[...])

    for r0 in range(0, tt, rc):
        acc = jnp.zeros((rc, C), F32)
        for k in range(taps):
            acc = acc + w_ref[k:k + 1, :] * ap_ref[r0 + k + off:r0 + k + off + rc, :]
        c = acc + cb_ref[...]
        mu = jnp.mean(c, axis=-1, keepdims=True)
        d = c - mu
        var = jnp.mean(d * d, axis=-1, keepdims=True)
        yl = (d * lax.rsqrt(var + EPS)) * lg_ref[...] + lb_ref[...]
        y_ref[r0:r0 + rc, :] = (_silu(yl) * _silu(z_ref[r0:r0 + rc, :])).astype(y_ref.dtype)

    @pl.when(t == nt - 1)
    def _():
        cn_ref[...] = ap_ref[tt + off:tt + hist, :]


def _conv_prompt(proj, conv_buf, conv_w, conv_b, ln_g, ln_b, l, *, B, T, C, tt):
    taps = conv_w.shape[1]
    nt = T // tt
    hist = -(-(taps - 1) // 8) * 8
    vec = pl.BlockSpec((None, 1, C), lambda b, t: (l, 0, 0))
    return pl.pallas_call(
        functools.partial(_conv_body, tt=tt, nt=nt, taps=taps, rc=32),
        grid=(B, nt),
        in_specs=[
            pl.BlockSpec((tt, C), lambda b, t: (b * nt + t, 0)),
            pl.BlockSpec((tt, C), lambda b, t: (b * nt + t, 1)),
            pl.BlockSpec((tt, C), lambda b, t: (b * nt + t, 2)),
            pl.BlockSpec((None, taps - 1, C), lambda b, t: (b, 0, 0)),
            pl.BlockSpec((None, taps, C), lambda b, t: (l, 0, 0)),
            vec, vec, vec,
        ],
        out_specs=[
            pl.BlockSpec((tt, C), lambda b, t: (b * nt + t, 0)),
            pl.BlockSpec((None, taps - 1, C), lambda b, t: (b, 0, 0)),
        ],
        out_shape=[jax.ShapeDtypeStruct((B * T, C), BF16),
                   jax.ShapeDtypeStruct((B, taps - 1, C), F32)],
        scratch_shapes=[pltpu.VMEM((tt + hist, C), F32)],
        compiler_params=_cparams(("parallel", "arbitrary")),
        name="conv_prompt",
    )(proj, proj, proj, conv_buf, conv_w, conv_b, ln_g, ln_b)


def _rope_tables(pos, head_dim):
    rope_dim = head_dim // 4
    half = rope_dim // 2
    inv = ROPE_THETA ** (-jnp.arange(half, dtype=F32) / half)
    ang = pos.astype(F32)[:, None] * inv[None, :]
    cos, sin = jnp.cos(ang), jnp.sin(ang)
    d = jnp.arange(LANES) % head_dim
    f = d % half
    cosl, sinl = cos[:, f], sin[:, f]
    c = jnp.where(d < rope_dim, cosl, 1.0)
    a = jnp.where(d < half, -sinl, 0.0)
    b = jnp.where((d >= half) & (d < rope_dim), sinl, 0.0)
    return c.astype(F32), a.astype(F32), b.astype(F32)


def _rope(x, c, a, b, half):
    return x * c + pltpu.roll(x, LANES - half, 1) * a + pltpu.roll(x, half, 1) * b


def _dup_head(x, odd, lo):
    r = pltpu.roll(x, LANES // 2, 1)
    return jnp.where(lo, r, x) if odd else jnp.where(lo, x, r)


def _assemble_body(a_ref, b_ref, c_ref, gl_ref, gb_ref, tc_ref, ta_ref, tb_ref,
                   kvn_ref, win_ref, ks_ref, vs_ref, kw_ref, vw_ref, gt_ref, *, n_kv, half, gpg):
    tc, ta, tb = tc_ref[...], ta_ref[...], tb_ref[...]
    tt = tc.shape[0]
    lo = lax.broadcasted_iota(jnp.int32, (tt, LANES), 1) < LANES // 2
    kvw = n_kv * LANES // 2
    ncol = kvw // LANES

    a = a_ref[...]
    kvn_ref[:, 0:2 * kvw] = a
    b = b_ref[...]
    c = c_ref[...]
    ksr, kwr = [], []
    for j in range(ncol):
        ksr.append(_rope(b[:, j * LANES:(j + 1) * LANES], tc, ta, tb, half))
        kwr.append(_rope(c[:, j * LANES:(j + 1) * LANES], tc, ta, tb, half))
        kvn_ref[:, 2 * kvw + j * LANES:2 * kvw + (j + 1) * LANES] = ksr[j]
        win_ref[:, j * LANES:(j + 1) * LANES] = kwr[j]
    kvn_ref[:, 3 * kvw:4 * kvw] = b[:, kvw:2 * kvw]
    win_ref[:, kvw:2 * kvw] = c[:, kvw:2 * kvw]

    g = jax.nn.sigmoid(gl_ref[...] + gb_ref[...])
    for h in range(n_kv):
        j, odd = h // 2, h % 2
        ks_ref[h] = _dup_head(ksr[j], odd, lo).astype(BF16)
        vs_ref[h] = _dup_head(b[:, kvw + j * LANES:kvw + (j + 1) * LANES], odd, lo).astype(BF16)
        kw_ref[h] = _dup_head(kwr[j], odd, lo).astype(BF16)
        vw_ref[h] = _dup_head(c[:, kvw + j * LANES:kvw + (j + 1) * LANES], odd, lo).astype(BF16)
        gt_ref[h] = g if h == 0 else pltpu.roll(g, LANES - gpg * h, 1)


def _assemble_prompt(proj, gl, gate_b, tabs, l, *, B, T, n_kv, head_dim, kv_col, tt, gpg):
    kvw = n_kv * head_dim
    nt = T // tt
    cb = kv_col // (2 * kvw)
    row = lambda b, t: b * nt + t
    tab = pl.BlockSpec((tt, LANES), lambda b, t: (t, 0))
    exp = pl.BlockSpec((None, n_kv, tt, LANES), lambda b, t: (b, 0, t, 0))
    exp_shape = jax.ShapeDtypeStruct((B, n_kv, T, LANES), BF16)
    return pl.pallas_call(
        functools.partial(_assemble_body, n_kv=n_kv, half=head_dim // 8, gpg=gpg),
        grid=(B, nt),
        in_specs=[
            pl.BlockSpec((tt, 2 * kvw), lambda b, t: (row(b, t), cb)),
            pl.BlockSpec((tt, 2 * kvw), lambda b, t: (row(b, t), cb + 1)),
            pl.BlockSpec((tt, 2 * kvw), lambda b, t: (row(b, t), cb + 2)),
            pl.BlockSpec((tt, LANES), lambda b, t: (row(b, t), 0)),
            pl.BlockSpec((None, 1, LANES), lambda b, t: (l, 0, 0)),
            tab, tab, tab,
        ],
        out_specs=[
            pl.BlockSpec((tt, 4 * kvw), lambda b, t: (row(b, t), 0)),
            pl.BlockSpec((tt, 2 * kvw), lambda b, t: (row(b, t), 0)),
            exp, exp, exp, exp,
            pl.BlockSpec((None, n_kv, tt, LANES), lambda b, t: (b, 0, t, 0)),
        ],
        out_shape=[
            jax.ShapeDtypeStruct((B * T, 4 * kvw), F32),
            jax.ShapeDtypeStruct((B * T, 2 * kvw), F32),
            exp_shape, exp_shape, exp_shape, exp_shape,
            jax.ShapeDtypeStruct((B, n_kv, T, LANES), F32),
        ],
        compiler_params=_cparams(("parallel", "parallel")),
        name="assemble_prompt",
    )(proj, proj, proj, gl, gate_b, *tabs)


def _compress_body(pidx_ref, src_ref, pos_ref, w1_ref, b1_ref, w2_ref, b2_ref, out_ref,
                   xbuf, acc_ref, sem, *, P, JC, NJC, G, col0, cw):
    g = pl.program_id(0)
    jc = pl.program_id(1)
    step = g * NJC + jc
    slot = lax.rem(step, 2)

    def page_copy(pg, j0, sl, p):
        return pltpu.make_async_copy(
            src_ref.at[pg, :, :, pl.ds(j0, JC), pl.ds(col0, cw)],
            xbuf.at[sl, :, :, :, p, :], sem.at[sl])

    def issue(st, sl):
        gg = st // NJC
        j0 = lax.rem(st, NJC) * JC

        def body(p, carry):
            page_copy(pidx_ref[gg * P + p], j0, sl, p).start()
            return carry
        lax.fori_loop(0, P, body, 0)

    @pl.when(step == 0)
    def _():
        issue(step, slot)

    @pl.when(step + 1 < G * NJC)
    def _():
        issue(step + 1, 1 - slot)

    def wait_body(p, carry):
        page_copy(0, 0, slot, p).wait()
        return carry
    lax.fori_loop(0, P, wait_body, 0)

    @pl.when(jc == 0)
    def _():
        acc_ref[...] = jnp.zeros(acc_ref.shape, F32)

    hw = cw // 2
    for c in range(2):
        tot = None
        for j in range(JC):
            parts = []
            for p2 in range(hw // LANES):
                col = c * hw + p2 * LANES
                for par in range(2):
                    lo = xbuf[slot, par, 0, j, :, col:col + LANES] + pos_ref[c, 0, j]
                    hi = xbuf[slot, par, 1, j, :, col:col + LANES] + pos_ref[c, 1, j]
                    parts.append(jnp.concatenate([lo, hi], axis=1))
            lhs = jnp.concatenate(parts, axis=0).astype(BF16)
            res = jnp.dot(lhs, w1_ref[c, j], preferred_element_type=F32)
            tot = res if tot is None else tot + res
        acc_ref[c] += tot

    @pl.when(jc == NJC - 1)
    def _():
        for c in range(2):
            hid = _silu(acc_ref[c] + b1_ref[c])
            comp = jnp.dot(hid.astype(BF16), w2_ref[c], preferred_element_type=F32) + b2_ref[c]
            for p2 in range(hw // LANES):
                for hl in range(2):
                    for par in range(2):
                        r0 = (p2 * 2 + par) * P
                        out_ref[c, 2 * p2 + hl, par] = (
                            comp[r0:r0 + P, hl * LANES:(hl + 1) * LANES].astype(BF16))


def _compress(page_idx, src, cmpw, l, *, G, P, col0, n_kv, head_dim):
    pos_e, w1_e, b1_e, w2_e, b2_e = cmpw
    half_rows = src.shape[3]
    JC = 8
    NJC = half_rows // JC
    cw = 2 * n_kv * head_dim
    nrow = 2 * (cw // 2 // LANES) * P
    grid_spec = pltpu.PrefetchScalarGridSpec(
        num_scalar_prefetch=1,
        grid=(G, NJC),
        in_specs=[
            pl.BlockSpec(memory_space=pl.ANY),
            pl.BlockSpec((None, 2, 2, JC, 1, LANES), lambda g, j, p: (l, 0, 0, j, 0, 0)),
            pl.BlockSpec((None, 2, JC, 2 * LANES, 2 * LANES), lambda g, j, p: (l, 0, j, 0, 0)),
            pl.BlockSpec((None, 2, 1, 2 * LANES), lambda g, j, p: (l, 0, 0, 0)),
            pl.BlockSpec((None, 2, 2 * LANES, 2 * LANES), lambda g, j, p: (l, 0, 0, 0)),
            pl.BlockSpec((None, 2, 1, 2 * LANES), lambda g, j, p: (l, 0, 0, 0)),
        ],
        out_specs=pl.BlockSpec((2, n_kv, 2, P, LANES), lambda g, j, p: (0, 0, 0, g, 0)),
        scratch_shapes=[
            pltpu.VMEM((2, 2, 2, JC, P, cw), F32),
            pltpu.VMEM((2, nrow, 2 * LANES), F32),
            pltpu.SemaphoreType.DMA((2,)),
        ],
    )
    return pl.pallas_call(
        functools.partial(_compress_body, P=P, JC=JC, NJC=NJC, G=G, col0=col0, cw=cw),
        grid_spec=grid_spec,
        out_shape=jax.ShapeDtypeStruct((2, n_kv, 2, G * P, LANES), BF16),
        compiler_params=_cparams(("arbitrary", "arbitrary")),
        name="compress",
    )(page_idx, src, pos_e, w1_e, b1_e, w2_e, b2_e)


def _compress_weights(cmp_pos, cmp_w1, cmp_b1, cmp_w2, cmp_b2):
    depth, _, blk, hd = cmp_pos.shape
    hidden = cmp_w1.shape[-1]
    hr = blk // 2
    eye = jnp.eye(2, dtype=F32)
    pos_e = jnp.tile(cmp_pos.reshape(depth, 2, 2, hr, 1, hd), (1, 1, 1, 1, 1, 2))
    w1 = cmp_w1.reshape(depth, 2, 2, hr, hd, hidden)
    w1_e = jnp.einsum("lcajde,hk->lcjahdke", w1, eye).reshape(depth, 2, hr, 4 * hd, 2 * hidden)
    b1_e = jnp.tile(cmp_b1.reshape(depth, 2, 1, hidden), (1, 1, 1, 2))
    w2_e = jnp.einsum("lced,hk,u->lchekud", cmp_w2, eye, jnp.ones((2,), F32))
    w2_e = w2_e.reshape(depth, 2, 2 * hidden, 4 * hd)
    b2_e = jnp.tile(cmp_b2.reshape(depth, 2, 1, hd), (1, 1, 1, 4))
    return pos_e, w1_e.astype(BF16), b1_e, w2_e.astype(BF16), b2_e


def _stack_heads(cols, lo):
    rows = []
    for cj in cols:
        rows.append(jnp.where(lo, cj, 0.0))
        rows.append(jnp.where(lo, 0.0, cj))
    return jnp.concatenate(rows, axis=0).astype(BF16)


def _block_ids(n, pb):
    return jnp.where(n >= pb, 2 * (n - pb) + 1, 2 * n)


def _nsa_prompt_body(q_ref, z_ref, gt_ref, tc_ref, ta_ref, tb_ref, ks_ref, vs_ref, kw_ref, vw_ref,
                     kc_ref, vc_ref, o_ref, *, tq, tk, T, PB, blk, half, scale):
    qi = pl.program_id(2)
    t0 = qi * tq
    nb = 2 * PB
    shift = blk.bit_length() - 1
    tc, ta, tb = tc_ref[...], ta_ref[...], tb_ref[...]
    lo = lax.broadcasted_iota(jnp.int32, (tq, LANES), 1) < LANES // 2
    q = q_ref[...] * scale
    q0, q1 = q[:, 0:LANES], q[:, LANES:2 * LANES]
    qc = _stack_heads([q0, q1], lo)
    qr = _stack_heads([_rope(q0, tc, ta, tb, half), _rope(q1, tc, ta, tb, half)], lo)
    tpos = t0 + lax.broadcasted_iota(jnp.int32, (tq, 1), 0)
    cur = lax.shift_right_logical(tpos, shift)

    def tile4(x):
        return jnp.concatenate([x, x, x, x], axis=0)

    kc = jnp.concatenate([kc_ref[0], kc_ref[1]], axis=0)
    vc = jnp.concatenate([vc_ref[0], vc_ref[1]], axis=0)
    bid = _block_ids(lax.broadcasted_iota(jnp.int32, (tq, nb), 1), PB)
    cm4 = tile4(((bid + 1) * blk - 1 <= tpos).astype(F32)) > 0.5
    sc = lax.dot_general(qc, kc, NT_DIMS, preferred_element_type=F32)
    sc = jnp.where(cm4, sc, NEG)
    mc = jnp.max(sc, axis=-1, keepdims=True)
    ec = jnp.where(cm4, jnp.exp(sc - mc), 0.0)
    pc = ec / jnp.maximum(jnp.sum(ec, axis=-1, keepdims=True), 1e-30)
    oc = jnp.dot(pc.astype(BF16), vc, preferred_element_type=F32)
    imp = pc[0:tq] + pc[tq:2 * tq] + pc[2 * tq:3 * tq] + pc[3 * tq:4 * tq]

    forced = (bid == 0) | (bid == cur) | (bid == cur - 1)
    valid = bid <= cur
    score = jnp.where(valid, jnp.where(forced, BIG, imp), -BIG)
    cnt = jnp.zeros((tq, nb), F32)
    for i in range(nb):
        col = score[:, i:i + 1]
        bi = 2 * (i % PB) + i // PB
        beats = (col > score) | ((col == score) & (bid > bi))
        cnt = cnt + beats.astype(F32)
    sel = ((cnt < TOP_N) & valid).astype(BF16)

    bid_col = _block_ids(lax.broadcasted_iota(jnp.int32, (nb, tk), 0), PB)
    key_l = lax.broadcasted_iota(jnp.int32, (nb, tk), 1)
    key_q = lax.broadcasted_iota(jnp.int32, (tq, tk), 1)

    def sel_chunk(ci, carry):
        m, l, acc = carry
        k0 = pl.multiple_of(ci * tk, tk)
        s = lax.dot_general(qr, ks_ref[pl.ds(k0, tk), :], NT_DIMS, preferred_element_type=F32)
        expand = (bid_col == lax.shift_right_logical(key_l + k0, shift)).astype(BF16)
        bm = jnp.dot(sel, expand, preferred_element_type=F32)
        ok = (bm > 0.5) & (key_q + k0 <= tpos)
        s = s + tile4(jnp.where(ok, 0.0, NEG))
        m_new = jnp.maximum(m, jnp.max(s, axis=-1, keepdims=True))
        alpha = jnp.exp(m - m_new)
        p = jnp.exp(s - m_new)
        l = alpha * l + jnp.sum(p, axis=-1, keepdims=True)
        acc = alpha * acc + jnp.dot(p.astype(BF16), vs_ref[pl.ds(k0, tk), :],
                                    preferred_element_type=F32)
        return m_new, l, acc

    nchunks = (t0 + tq + tk - 1) // tk
    init = (jnp.full((4 * tq, 1), NEG, F32), jnp.zeros((4 * tq, 1), F32),
            jnp.zeros((4 * tq, LANES), F32))
    _, ls, accs = lax.fori_loop(0, nchunks, sel_chunk, init)
    osel = accs / ls

    ww = min(WINDOW + tq, T)
    ws = pl.multiple_of(jnp.clip(t0 - WINDOW, 0, T - ww), tq)
    sw = lax.dot_general(qr, kw_ref[pl.ds(ws, ww), :], NT_DIMS, preferred_element_type=F32)
    dl = tpos - (ws + lax.broadcasted_iota(jnp.int32, (tq, ww), 1))
    sw = sw + tile4(jnp.where((dl >= 0) & (dl < WINDOW), 0.0, NEG))
    pw = jnp.exp(sw - jnp.max(sw, axis=-1, keepdims=True))
    ow = jnp.dot(pw.astype(BF16), vw_ref[pl.ds(ws, ww), :], preferred_element_type=F32)
    ow = ow / jnp.sum(pw, axis=-1, keepdims=True)

    g = gt_ref[...]

    def gcol(br):
        return jnp.concatenate([g[:, 3 * hh + br:3 * hh + br + 1] for hh in range(4)], axis=0)

    o = gcol(0) * oc + gcol(1) * osel + gcol(2) * ow
    p0 = jnp.where(lo, o[0:tq], o[tq:2 * tq])
    p1 = jnp.where(lo, o[2 * tq:3 * tq], o[3 * tq:4 * tq])
    o_ref[...] = (jnp.concatenate([p0, p1], axis=1) * _silu(z_ref[...])).astype(o_ref.dtype)


def _nsa_prompt(proj, gates, tabs, ks, vs, kw, vw, cmp_e, *, B, T, n_kv, head_dim, q_col, z_col,
                blk, tq, tk):
    nq = T // tq
    gw = 2 * LANES
    PB = T // (2 * blk)
    row = lambda b, h, i: b * nq + i
    tab = pl.BlockSpec((tq, LANES), lambda b, h, i: (i, 0))
    kvs = pl.BlockSpec((None, None, T, LANES), lambda b, h, i: (b, h, 0, 0))
    return pl.pallas_call(
        functools.partial(_nsa_prompt_body, tq=tq, tk=tk, T=T, PB=PB, blk=blk,
                          half=head_dim // 8, scale=head_dim ** -0.5),
        grid=(B, n_kv, nq),
        in_specs=[
            pl.BlockSpec((tq, gw), lambda b, h, i: (row(b, h, i), q_col // gw + h)),
            pl.BlockSpec((tq, gw), lambda b, h, i: (row(b, h, i), z_col // gw + h)),
            pl.BlockSpec((None, None, tq, LANES), lambda b, h, i: (b, h, i, 0)),
            tab, tab, tab,
            kvs, kvs, kvs, kvs,
            pl.BlockSpec((None, None, 2, PB, LANES), lambda b, h, i: (0, h, 0, b, 0)),
            pl.BlockSpec((None, None, 2, PB, LANES), lambda b, h, i: (1, h, 0, b, 0)),
        ],
        out_specs=pl.BlockSpec((tq, gw), lambda b, h, i: (row(b, h, i), h)),
        out_shape=jax.ShapeDtypeStruct((B * T, n_kv * gw), BF16),
        compiler_params=_cparams(("parallel", "parallel", "arbitrary")),
        name="nsa_prompt",
    )(proj, proj, gates, *tabs, ks, vs, kw, vw, cmp_e, cmp_e)


def _outproj_body(x_ref, a_ref, b_ref, wa_ref, wb_ref, g_ref, o_ref):
    acc = jnp.dot(a_ref[...].astype(BF16), wa_ref[...], preferred_element_type=F32)
    acc = acc + jnp.dot(b_ref[...].astype(BF16), wb_ref[...], preferred_element_type=F32)
    ms = jnp.mean(acc * acc, axis=-1, keepdims=True)
    o_ref[...] = x_ref[...] + (acc * lax.rsqrt(ms + EPS)) * g_ref[...]


def _outproj(x, a, b, w_out, g, l, *, tm):
    M, D = x.shape
    ka, kb = a.shape[1], b.shape[1]
    return pl.pallas_call(
        _outproj_body,
        grid=(M // tm,),
        in_specs=[
            pl.BlockSpec((tm, D), lambda i: (i, 0)),
            pl.BlockSpec((tm, ka), lambda i: (i, 0)),
            pl.BlockSpec((tm, kb), lambda i: (i, 0)),
            pl.BlockSpec((None, ka, D), lambda i: (l, 0, 0)),
            pl.BlockSpec((None, kb, D), lambda i: (l, ka // kb, 0)),
            pl.BlockSpec((None, 1, D), lambda i: (l, 0, 0)),
        ],
        out_specs=pl.BlockSpec((tm, D), lambda i: (i, 0)),
        out_shape=jax.ShapeDtypeStruct((M, D), F32),
        compiler_params=_cparams(("parallel",)),
        name="outproj",
    )(x, a, b, w_out, w_out, g)


def _sample_prep_body(p_ref, gl_ref, st_ref, w_ref, cb_ref, lg_ref, lb_ref, gb_ref,
                      tc_ref, ta_ref, tb_ref,
                      y_ref, cn_ref, kvn_ref, win_ref, qc_ref, qr_ref, gt_ref, zs_ref,
                      *, C, HD, kvw, half, scale):
    nb = p_ref.shape[0]
    taps = w_ref.shape[0]
    tc, ta, tb = tc_ref[...], ta_ref[...], tb_ref[...]
    glu = p_ref[:, 0:C] * jax.nn.sigmoid(p_ref[:, C:2 * C])
    rows = []
    for b in range(nb):
        st = st_ref[b]
        rows.append(jnp.sum(st * w_ref[0:taps - 1, :], axis=0, keepdims=True))
        cn_ref[b, 0:taps - 2, :] = st[1:taps - 1, :]
        cn_ref[b, taps - 2:taps - 1, :] = glu[b:b + 1, :]
    c = jnp.concatenate(rows, axis=0) + w_ref[taps - 1:taps, :] * glu + cb_ref[...]
    mu = jnp.mean(c, axis=-1, keepdims=True)
    d = c - mu
    var = jnp.mean(d * d, axis=-1, keepdims=True)
    yl = (d * lax.rsqrt(var + EPS)) * lg_ref[...] + lb_ref[...]
    y_ref[...] = _silu(yl) * _silu(p_ref[:, 2 * C:3 * C])

    q_col = 3 * C
    z_col = q_col + HD
    kv_col = z_col + HD
    for j in range(HD // LANES):
        qj = p_ref[:, q_col + j * LANES:q_col + (j + 1) * LANES] * scale
        qc_ref[:, j * LANES:(j + 1) * LANES] = qj
        qr_ref[:, j * LANES:(j + 1) * LANES] = _rope(qj, tc, ta, tb, half)
    zs_ref[...] = _silu(p_ref[:, z_col:z_col + HD])
    kvn_ref[:, 0:2 * kvw] = p_ref[:, kv_col:kv_col + 2 * kvw]
    kvn_ref[:, 3 * kvw:4 * kvw] = p_ref[:, kv_col + 3 * kvw:kv_col + 4 * kvw]
    win_ref[:, kvw:2 * kvw] = p_ref[:, kv_col + 5 * kvw:kv_col + 6 * kvw]
    for j in range(kvw // LANES):
        s0 = kv_col + 2 * kvw + j * LANES
        kvn_ref[:, 2 * kvw + j * LANES:2 * kvw + (j + 1) * LANES] = _rope(
            p_ref[:, s0:s0 + LANES], tc, ta, tb, half)
        w0 = kv_col + 4 * kvw + j * LANES
        win_ref[:, j * LANES:(j + 1) * LANES] = _rope(p_ref[:, w0:w0 + LANES], tc, ta, tb, half)
    gt_ref[...] = jax.nn.sigmoid(gl_ref[...] + gb_ref[...])


def _sample_prep(proj, gl, state_conv, conv_w, conv_b, ln_g, ln_b, gate_b, tabs, l,
                 *, C, HD, kvw, head_dim):
    nb = proj.shape[0]
    taps = conv_w.shape[1]
    whole = lambda shape: pl.BlockSpec(shape, lambda i: (0,) * len(shape))
    vecc = pl.BlockSpec((None, 1, C), lambda i: (l, 0, 0))
    tab = whole((1, LANES))
    shapes = [
        ((nb, C), F32), ((nb, taps - 1, C), F32), ((nb, 4 * kvw), F32), ((nb, 2 * kvw), F32),
        ((nb, HD), F32), ((nb, HD), F32), ((nb, LANES), F32), ((nb, HD), F32),
    ]
    return pl.pallas_call(
        functools.partial(_sample_prep_body, C=C, HD=HD, kvw=kvw, half=head_dim // 8,
                          scale=head_dim ** -0.5),
        grid=(1,),
        in_specs=[
            whole(proj.shape), whole(gl.shape),
            pl.BlockSpec((None, nb, taps - 1, C), lambda i: (l, 0, 0, 0)),
            pl.BlockSpec((None, taps, C), lambda i: (l, 0, 0)),
            vecc, vecc, vecc,
            pl.BlockSpec((None, 1, LANES), lambda i: (l, 0, 0)),
            tab, tab, tab,
        ],
        out_specs=[whole(s) for s, _ in shapes],
        out_shape=[jax.ShapeDtypeStruct(s, d) for s, d in shapes],
        compiler_params=_cparams(("arbitrary",)),
        name="sample_prep",
    )(proj, gl, state_conv, conv_w, conv_b, ln_g, ln_b, gate_b, *tabs)


def _sample_cmp_body(qc_ref, kc_ref, vc_ref, oc_ref, idx_ref, *, P, n_kv):
    b = pl.program_id(0)
    nb = 2 * P
    qrow = qc_ref[pl.ds(b, 1), :]
    lo = lax.broadcasted_iota(jnp.int32, (1, LANES), 1) < LANES // 2
    zero4 = jnp.zeros((4, LANES), F32)
    imps = []
    for h in range(n_kv):
        c0 = qrow[:, 2 * h * LANES:(2 * h + 1) * LANES]
        c1 = qrow[:, (2 * h + 1) * LANES:(2 * h + 2) * LANES]
        qs = jnp.concatenate([jnp.where(lo, c0, 0.0), jnp.where(lo, 0.0, c0),
                              jnp.where(lo, c1, 0.0), jnp.where(lo, 0.0, c1), zero4],
                             axis=0).astype(BF16)
        kc = jnp.concatenate([kc_ref[h, 0], kc_ref[h, 1]], axis=0)
        vc = jnp.concatenate([vc_ref[h, 0], vc_ref[h, 1]], axis=0)
        s = lax.dot_general(qs, kc, NT_DIMS, preferred_element_type=F32)
        e = jnp.exp(s - jnp.max(s, axis=-1, keepdims=True))
        pc = e / jnp.maximum(jnp.sum(e, axis=-1, keepdims=True), 1e-30)
        oc_ref[h] = jnp.dot(pc.astype(BF16), vc, preferred_element_type=F32)
        imps.append(pc[0:1] + pc[1:2] + pc[2:3] + pc[3:4])
    imp = jnp.concatenate(imps, axis=0)
    bid = _block_ids(lax.broadcasted_iota(jnp.int32, (n_kv, nb), 1), P)
    forced = (bid == 0) | (bid == nb - 1)
    score = jnp.where(forced, BIG, imp)
    cnt = (score < BIG).astype(F32)
    for i in range(nb):
        col = score[:, i:i + 1]
        bi = 2 * (i % P) + i // P
        beats = (col > score) | ((col == score) & (bid > bi))
        cnt = cnt + beats.astype(F32)
    lane = lax.broadcasted_iota(jnp.int32, (n_kv, LANES), 1)
    res = jnp.full((n_kv, LANES), -1.0, F32)
    bid1 = (bid + 1).astype(F32)
    for k in range(TOP_N):
        v = jnp.sum(jnp.where(cnt == k, bid1, 0.0), axis=-1, keepdims=True) - 1.0
        res = jnp.where(lane == k, v, res)
    idx_ref[...] = res.astype(jnp.int32)


def _sample_cmp(qc, cmp_e, *, DB, P, n_kv):
    ce = lambda c: pl.BlockSpec((None, n_kv, 2, P, LANES), lambda b: (c, 0, 0, b, 0))
    return pl.pallas_call(
        functools.partial(_sample_cmp_body, P=P, n_kv=n_kv),
        grid=(DB,),
        in_specs=[pl.BlockSpec(qc.shape, lambda b: (0, 0)), ce(0), ce(1)],
        out_specs=[pl.BlockSpec((None, n_kv, 8, LANES), lambda b: (b, 0, 0, 0)),
                   pl.BlockSpec((None, n_kv, LANES), lambda b: (b, 0, 0))],
        out_shape=[jax.ShapeDtypeStruct((DB, n_kv, 8, LANES), F32),
                   jax.ShapeDtypeStruct((DB, n_kv, LANES), jnp.int32)],
        compiler_params=_cparams(("parallel",)),
        name="sample_cmp",
    )(qc, cmp_e, cmp_e)


def _sample_attn_body(idx_ref, pages_ref, qr_ref, gt_ref, zs_ref, oc_ref, kvn_ref, wr_ref, cw_ref,
                      cache_ref, o_ref, wn_ref, selbuf, sem, *, P, n_kv, hd, kvw, nwin):
    b = pl.program_id(0)

    def blk_copy(src_blk, slot):
        return pltpu.make_async_copy(cache_ref.at[src_blk, :, pl.ds(2 * kvw, 2 * kvw)],
                                     selbuf.at[slot], sem.at[0])

    for h in range(n_kv):
        for k in range(TOP_N):
            ix = jnp.maximum(idx_ref[(b * n_kv + h) * LANES + k], 0)
            pg = pages_ref[b * P + lax.shift_right_logical(ix, 1)]
            blk_copy(pg * 2 + (ix & 1), h * TOP_N + k).start()

    wn_ref[0:nwin - 1, :] = cw_ref[1:nwin, :]
    wn_ref[nwin - 1:nwin, :] = wr_ref[pl.ds(b, 1), :]

    for s in range(n_kv * TOP_N):
        blk_copy(0, s).wait()

    qrow = qr_ref[pl.ds(b, 1), :]
    kvrow = kvn_ref[pl.ds(b, 1), :]
    g = gt_ref[pl.ds(b, 1), :]
    zero4 = jnp.zeros((4, hd), F32)
    zero41 = jnp.zeros((4, 1), F32)
    pieces = []
    for h in range(n_kv):
        qh = jnp.concatenate([qrow[:, (4 * h + gg) * hd:(4 * h + gg + 1) * hd] for gg in range(4)]
                             + [zero4], axis=0)
        qb = qh.astype(BF16)
        ks_new = kvrow[:, 2 * kvw + h * hd:2 * kvw + (h + 1) * hd]
        vs_new = kvrow[:, 3 * kvw + h * hd:3 * kvw + (h + 1) * hd]
        ss = []
        for k in range(TOP_N):
            kb = selbuf[h * TOP_N + k, :, h * hd:(h + 1) * hd].astype(BF16)
            sk = lax.dot_general(qb, kb, NT_DIMS, preferred_element_type=F32)
            live = idx_ref[(b * n_kv + h) * LANES + k] >= 0
            ss.append(sk + jnp.where(live, 0.0, NEG))
        s_all = jnp.concatenate(ss, axis=1)
        s_new = jnp.sum(qb.astype(F32) * ks_new.astype(BF16).astype(F32), axis=-1, keepdims=True)
        m = jnp.maximum(jnp.max(s_all, axis=-1, keepdims=True), s_new)
        e = jnp.exp(s_all - m)
        e_new = jnp.exp(s_new - m)
        lsum = jnp.sum(e, axis=-1, keepdims=True) + e_new
        acc = e_new.astype(BF16).astype(F32) * vs_new.astype(BF16).astype(F32)
        nk = selbuf.shape[1]
        for k in range(TOP_N):
            vb = selbuf[h * TOP_N + k, :, kvw + h * hd:kvw + (h + 1) * hd].astype(BF16)
            acc = acc + jnp.dot(e[:, k * nk:(k + 1) * nk].astype(BF16), vb,
                                preferred_element_type=F32)
        osel = acc / lsum

        kwn = wn_ref[:, h * hd:(h + 1) * hd].astype(BF16)
        vwn = wn_ref[:, kvw + h * hd:kvw + (h + 1) * hd].astype(BF16)
        sw = lax.dot_general(qb, kwn, NT_DIMS, preferred_element_type=F32)
        pw = jnp.exp(sw - jnp.max(sw, axis=-1, keepdims=True))
        ow = jnp.dot(pw.astype(BF16), vwn, preferred_element_type=F32)
        ow = ow / jnp.sum(pw, axis=-1, keepdims=True)

        def gcol(br):
            return jnp.concatenate(
                [g[:, (4 * h + gg) * 3 + br:(4 * h + gg) * 3 + br + 1] for gg in range(4)]
                + [zero41], axis=0)

        o = gcol(0) * oc_ref[h][:, 0:hd] + gcol(1) * osel + gcol(2) * ow
        for gg in range(4):
            pieces.append(o[gg:gg + 1, :])
    o_ref[pl.ds(b, 1), :] = jnp.concatenate(pieces, axis=1) * zs_ref[pl.ds(b, 1), :]


def _sample_attn(idx, pages, qr, gates, zs, oc, kvn, winrow, cache_win, cache_blk, l,
                 *, DB, P, n_kv, head_dim):
    kvw = n_kv * head_dim
    nwin = cache_win.shape[1]
    blk_rows = cache_blk.shape[1]
    whole = lambda a: pl.BlockSpec(a.shape, lambda b, i, p: (0,) * a.ndim)
    grid_spec = pltpu.PrefetchScalarGridSpec(
        num_scalar_prefetch=2,
        grid=(DB,),
        in_specs=[
            whole(qr), whole(gates), whole(zs),
            pl.BlockSpec((None, n_kv, 8, LANES), lambda b, i, p: (b, 0, 0, 0)),
            whole(kvn), whole(winrow),
            pl.BlockSpec((None, nwin, 2 * kvw), lambda b, i, p: (l * DB + b, 0, 0)),
            pl.BlockSpec(memory_space=pl.ANY),
        ],
        out_specs=[
            pl.BlockSpec(qr.shape, lambda b, i, p: (0, 0)),
            pl.BlockSpec((None, nwin, 2 * kvw), lambda b, i, p: (b, 0, 0)),
        ],
        scratch_shapes=[
            pltpu.VMEM((n_kv * TOP_N, blk_rows, 2 * kvw), F32),
            pltpu.SemaphoreType.DMA((1,)),
        ],
    )
    return pl.pallas_call(
        functools.partial(_sample_attn_body, P=P, n_kv=n_kv, hd=head_dim, kvw=kvw, nwin=nwin),
        grid_spec=grid_spec,
        out_shape=[jax.ShapeDtypeStruct(qr.shape, F32),
                   jax.ShapeDtypeStruct((DB, nwin, 2 * kvw), F32)],
        compiler_params=_cparams(("arbitrary",)),
        name="sample_attn",
    )(idx, pages, qr, gates, zs, oc, kvn, winrow, cache_win, cache_blk)


def kernel(x_prompt, x_sample, cache_kv_pages, cache_win, state_conv, page_table, w_in, w_out,
           norm_pre, norm_post, conv_w, conv_b, conv_ln_g, conv_ln_b, cmp_pos, cmp_w1, cmp_b1,
           cmp_w2, cmp_b2, gate_b):
    B, S, D = x_prompt.shape
    DB, dec_seq, _ = x_sample.shape
    depth, n_pool, page, _, n_kv, head_dim = cache_kv_pages.shape
    n_pages = page_table.shape[1]
    past_len = n_pages * page
    C = conv_w.shape[-1]
    taps = conv_w.shape[1]
    n_heads = gate_b.shape[-1] // 3
    HD = n_heads * head_dim
    kvw = n_kv * head_dim
    blk = cmp_pos.shape[2]
    n_main = 3 * C + 2 * HD + 6 * kvw
    nwin = cache_win.shape[2]
    gpg = 3 * (n_heads // n_kv)
    assert dec_seq == 1 and page == 2 * blk and past_len % blk == 0
    assert w_in.shape[-1] == n_main + 3 * n_heads and n_heads == 4 * n_kv and head_dim * 2 == LANES
    assert nwin == WINDOW and S >= WINDOW and past_len // blk >= TOP_N and S % page == 0

    w_main = w_in[:, :, :n_main].astype(BF16)
    w_gate = jnp.pad(w_in[:, :, n_main:], ((0, 0), (0, 0), (0, LANES - 3 * n_heads))).astype(BF16)
    w_out_b = w_out.astype(BF16)
    gate_bp = jnp.pad(gate_b, ((0, 0), (0, LANES - 3 * n_heads))).reshape(depth, 1, LANES)
    cmpw = _compress_weights(cmp_pos, cmp_w1, cmp_b1, cmp_w2, cmp_b2)
    vec3 = lambda v: v.reshape(depth, 1, v.shape[-1])
    norm_pre3, norm_post3 = vec3(norm_pre), vec3(norm_post)
    conv_b3, ln_g3, ln_b3 = vec3(conv_b), vec3(conv_ln_g), vec3(conv_ln_b)

    tabs_p = _rope_tables(jnp.arange(S, dtype=jnp.int32), head_dim)
    tabs_s = _rope_tables(jnp.full((1,), past_len, jnp.int32), head_dim)

    cache_rows = cache_kv_pages.reshape(depth * n_pool, 2, 2, blk // 2, 4 * kvw)
    cache_blk = cache_rows.reshape(depth * n_pool * 2, blk, 4 * kvw)
    cache_win2 = cache_win.reshape(depth * DB, nwin, 2 * kvw)
    pages_prompt = jnp.arange(B * S // page, dtype=jnp.int32)
    conv_zero = jnp.zeros((B, taps - 1, C), F32)

    xp = x_prompt.reshape(B * S, D)
    xs = x_sample.reshape(DB, D)
    tm_in = 1024 if (B * S) % 1024 == 0 else 256
    tm_out = 512 if (B * S) % 512 == 0 else 256
    outs = [[] for _ in range(6)]
    for l in range(depth):
        proj, gl = _inproj(xp, norm_pre3, w_main, w_gate, l, tm=tm_in, tn=512)
        yconv, conv_new = _conv_prompt(proj, conv_zero, conv_w, conv_b3, ln_g3, ln_b3, l,
                                       B=B, T=S, C=C, tt=256)
        kvn, winr, ks, vs, kw, vw, gates = _assemble_prompt(
            proj, gl, gate_bp, tabs_p, l, B=B, T=S, n_kv=n_kv, head_dim=head_dim,
            kv_col=3 * C + 2 * HD, tt=256, gpg=gpg)
        cmp_e = _compress(pages_prompt, proj.reshape(B * S // page, 2, 2, blk // 2, n_main), cmpw, l,
                          G=1, P=B * S // page, col0=3 * C + 2 * HD, n_kv=n_kv, head_dim=head_dim)
        attn = _nsa_prompt(proj, gates, tabs_p, ks, vs, kw, vw, cmp_e, B=B, T=S, n_kv=n_kv,
                           head_dim=head_dim, q_col=3 * C, z_col=3 * C + HD, blk=blk, tq=128, tk=512)
        xp = _outproj(xp, yconv, attn, w_out_b, norm_post3, l, tm=tm_out)
        outs[0].append(kvn.reshape(B, S, 4, n_kv, head_dim))
        outs[2].append(winr.reshape(B, S, 2, n_kv, head_dim)[:, S - nwin:])
        outs[4].append(conv_new)

        proj_s, gl_s = _inproj(xs, norm_pre3, w_main, w_gate, l, tm=DB, tn=512)
        (yconv_s, conv_new_s, kvn_s, winrow_s, qc_s, qr_s, gates_s, zs_s) = _sample_prep(
            proj_s, gl_s, state_conv, conv_w, conv_b3, ln_g3, ln_b3, gate_bp, tabs_s, l,
            C=C, HD=HD, kvw=kvw, head_dim=head_dim)
        pages_s = (page_table + l * n_pool).reshape(-1).astype(jnp.int32)
        cmp_s = _compress(pages_s, cache_rows, cmpw, l, G=DB, P=n_pages, col0=0,
                          n_kv=n_kv, head_dim=head_dim)
        oc_s, idx_s = _sample_cmp(qc_s, cmp_s, DB=DB, P=n_pages, n_kv=n_kv)
        attn_s, win_new_s = _sample_attn(idx_s.reshape(-1), pages_s, qr_s, gates_s, zs_s, oc_s, kvn_s,
                                         winrow_s, cache_win2, cache_blk, l,
                                         DB=DB, P=n_pages, n_kv=n_kv, head_dim=head_dim)
        xs = _outproj(xs, yconv_s, attn_s, w_out_b, norm_post3, l, tm=DB)
        outs[1].append(kvn_s.reshape(DB, 1, 4, n_kv, head_dim))
        outs[3].append(win_new_s.reshape(DB, nwin, 2, n_kv, head_dim))
        outs[5].append(conv_new_s)

    return (xp.reshape(B, S, D), xs.reshape(DB, 1, D), jnp.stack(outs[0]), jnp.stack(outs[1]),
            jnp.stack(outs[2]), jnp.stack(outs[3]), jnp.stack(outs[4]), jnp.stack(outs[5]))
```

```python
import functools

import jax
import jax.numpy as jnp
from jax import lax
from jax.experimental import pallas as pl
from jax.experimental.pallas import tpu as pltpu

F32 = jnp.float32
BF16 = jnp.bfloat16

TOP_N = 16
WINDOW = 512
ROPE_THETA = 500000.0
EPS = 1e-6
BIG = 1e9
NEG = -1e30
LANES = 128
VMEM_LIMIT = 56 * 1024 * 1024

NT_DIMS = (((1,), (1,)), ((), ()))


def _silu(x):
    return x * jax.nn.sigmoid(x)


def _cparams(sem):
    return pltpu.CompilerParams(dimension_semantics=sem, vmem_limit_bytes=VMEM_LIMIT)


def _inproj_body(x_ref, g_ref, w_ref, o_ref, u_ref, *, n_in, tn):
    j = pl.program_id(1)

    @pl.when(j == 0)
    def _():
        x = x_ref[...]
        ms = jnp.mean(x * x, axis=-1, keepdims=True)
        u_ref[...] = ((x * lax.rsqrt(ms + EPS)) * g_ref[...]).astype(u_ref.dtype)

    col = j * tn + lax.broadcasted_iota(jnp.int32, (1, tn), 1)
    w = jnp.where(col < n_in, w_ref[...], 0.0).astype(BF16)
    o_ref[...] = jnp.dot(u_ref[...].astype(BF16), w, preferred_element_type=F32)


def _inproj(x, g, w_in, l, *, tm, tn):
    M, D = x.shape
    n_in = w_in.shape[-1]
    nj = pl.cdiv(n_in, tn)
    u_dtype = BF16 if tm % 16 == 0 else F32
    return pl.pallas_call(
        functools.partial(_inproj_body, n_in=n_in, tn=tn),
        grid=(M // tm, nj),
        in_specs=[
            pl.BlockSpec((tm, D), lambda i, j: (i, 0)),
            pl.BlockSpec((None, 1, D), lambda i, j: (l, 0, 0)),
            pl.BlockSpec((None, D, tn), lambda i, j: (l, 0, j)),
        ],
        out_specs=pl.BlockSpec((tm, tn), lambda i, j: (i, j)),
        out_shape=jax.ShapeDtypeStruct((M, nj * tn), F32),
        scratch_shapes=[pltpu.VMEM((tm, D), u_dtype)],
        compiler_params=_cparams(("parallel", "arbitrary")),
        name="inproj",
    )(x, g, w_in)


def _conv_body(av_ref, ag_ref, z_ref, buf_ref, w_ref, cb_ref, lg_ref, lb_ref,
               y_ref, cn_ref, ap_ref, zs_ref, cv_ref, *, tt, nt, taps, rc, cc):
    t = pl.program_id(1)
    hist = -(-(taps - 1) // 8) * 8
    off = hist - (taps - 1)
    C = ap_ref.shape[1]

    @pl.when(t == 0)
    def _():
        ap_ref[0:off, :] = jnp.zeros((off, C), F32)
        ap_ref[off:hist, :] = buf_ref[...]

    @pl.when(t > 0)
    def _():
        ap_ref[0:hist, :] = ap_ref[tt:tt + hist, :]

    ap_ref[hist:hist + tt, :] = av_ref[...] * jax.nn.sigmoid(ag_ref[...])

    nz = zs_ref.shape[1]
    for s in range(1, 8):
        zs_ref[s - 1] = ap_ref[s:s + nz, :]
    for r0 in range(0, tt, rc):
        for c0 in range(0, C, cc):
            acc = jnp.zeros((rc, cc), F32)
            for k in range(taps):
                s, a = (k + off) % 8, (k + off) // 8
                src = ap_ref if s == 0 else zs_ref.at[s - 1]
                acc = acc + w_ref[k:k + 1, c0:c0 + cc] * src[r0 + 8 * a:r0 + 8 * a + rc, c0:c0 + cc]
            cv_ref[r0:r0 + rc, c0:c0 + cc] = acc
        c = cv_ref[r0:r0 + rc, :] + cb_ref[...]
        mu = jnp.mean(c, axis=-1, keepdims=True)
        d = c - mu
        var = jnp.mean(d * d, axis=-1, keepdims=True)
        yl = (d * lax.rsqrt(var + EPS)) * lg_ref[...] + lb_ref[...]
        y_ref[r0:r0 + rc, :] = (_silu(yl) * _silu(z_ref[r0:r0 + rc, :])).astype(y_ref.dtype)

    @pl.when(t == nt - 1)
    def _():
        cn_ref[...] = ap_ref[tt + off:tt + hist, :]


def _conv_prompt(proj, conv_buf, conv_w, conv_b, ln_g, ln_b, l, *, B, T, C, tt):
    taps = conv_w.shape[1]
    nt = T // tt
    hist = -(-(taps - 1) // 8) * 8
    vec = pl.BlockSpec((None, 1, C), lambda b, t: (l, 0, 0))
    return pl.pallas_call(
        functools.partial(_conv_body, tt=tt, nt=nt, taps=taps, rc=32, cc=min(C, 4 * LANES)),
        grid=(B, nt),
        in_specs=[
            pl.BlockSpec((tt, C), lambda b, t: (b * nt + t, 0)),
            pl.BlockSpec((tt, C), lambda b, t: (b * nt + t, 1)),
            pl.BlockSpec((tt, C), lambda b, t: (b * nt + t, 2)),
            pl.BlockSpec((None, taps - 1, C), lambda b, t: (b, 0, 0)),
            pl.BlockSpec((None, taps, C), lambda b, t: (l, 0, 0)),
            vec, vec, vec,
        ],
        out_specs=[
            pl.BlockSpec((tt, C), lambda b, t: (b * nt + t, 0)),
            pl.BlockSpec((None, taps - 1, C), lambda b, t: (b, 0, 0)),
        ],
        out_shape=[jax.ShapeDtypeStruct((B * T, C), BF16),
                   jax.ShapeDtypeStruct((B, taps - 1, C), F32)],
        scratch_shapes=[pltpu.VMEM((tt + hist, C), F32), pltpu.VMEM((7, tt + hist - 8, C), F32),
                        pltpu.VMEM((tt, C), F32)],
        compiler_params=_cparams(("parallel", "arbitrary")),
        name="conv_prompt",
    )(proj, proj, proj, conv_buf, conv_w, conv_b, ln_g, ln_b)


def _rope_tables(pos, head_dim):
    rope_dim = head_dim // 4
    half = rope_dim // 2
    inv = ROPE_THETA ** (-jnp.arange(half, dtype=F32) / half)
    ang = pos.astype(F32)[:, None] * inv[None, :]
    cos, sin = jnp.cos(ang), jnp.sin(ang)
    d = jnp.arange(LANES) % head_dim
    f = d % half
    cosl, sinl = cos[:, f], sin[:, f]
    c = jnp.where(d < rope_dim, cosl, 1.0)
    a = jnp.where(d < half, -sinl, 0.0)
    b = jnp.where((d >= half) & (d < rope_dim), sinl, 0.0)
    return c.astype(F32), a.astype(F32), b.astype(F32)


def _rope(x, c, a, b, half):
    return x * c + pltpu.roll(x, LANES - half, 1) * a + pltpu.roll(x, half, 1) * b


def _dup_head(x, odd, lo):
    r = pltpu.roll(x, LANES // 2, 1)
    return jnp.where(lo, r, x) if odd else jnp.where(lo, x, r)


def _assemble_body(a_ref, b_ref, c_ref, gl_ref, gb_ref, tc_ref, ta_ref, tb_ref,
                   kvn_ref, win_ref, ks_ref, vs_ref, kw_ref, vw_ref, gt_ref, *, n_kv, half, gpg):
    tc, ta, tb = tc_ref[...], ta_ref[...], tb_ref[...]
    tt = tc.shape[0]
    lo = lax.broadcasted_iota(jnp.int32, (tt, LANES), 1) < LANES // 2
    kvw = n_kv * LANES // 2
    ncol = kvw // LANES

    a = a_ref[...]
    kvn_ref[:, 0:2 * kvw] = a
    b = b_ref[...]
    c = c_ref[...]
    ksr, kwr = [], []
    for j in range(ncol):
        ksr.append(_rope(b[:, j * LANES:(j + 1) * LANES], tc, ta, tb, half))
        kwr.append(_rope(c[:, j * LANES:(j + 1) * LANES], tc, ta, tb, half))
        kvn_ref[:, 2 * kvw + j * LANES:2 * kvw + (j + 1) * LANES] = ksr[j]
        win_ref[:, j * LANES:(j + 1) * LANES] = kwr[j]
    kvn_ref[:, 3 * kvw:4 * kvw] = b[:, kvw:2 * kvw]
    win_ref[:, kvw:2 * kvw] = c[:, kvw:2 * kvw]

    g = jax.nn.sigmoid(gl_ref[...] + gb_ref[...])
    for h in range(n_kv):
        j, odd = h // 2, h % 2
        ks_ref[h] = _dup_head(ksr[j], odd, lo).astype(BF16)
        vs_ref[h] = _dup_head(b[:, kvw + j * LANES:kvw + (j + 1) * LANES], odd, lo).astype(BF16)
        kw_ref[h] = _dup_head(kwr[j], odd, lo).astype(BF16)
        vw_ref[h] = _dup_head(c[:, kvw + j * LANES:kvw + (j + 1) * LANES], odd, lo).astype(BF16)
        gt_ref[h] = g if h == 0 else pltpu.roll(g, LANES - gpg * h, 1)


def _assemble_prompt(proj, gate_b, tabs, l, *, B, T, n_kv, head_dim, kv_col, gl_col, tt, gpg):
    kvw = n_kv * head_dim
    nt = T // tt
    cb = kv_col // (2 * kvw)
    row = lambda b, t: b * nt + t
    tab = pl.BlockSpec((tt, LANES), lambda b, t: (t, 0))
    exp = pl.BlockSpec((None, n_kv, tt, LANES), lambda b, t: (b, 0, t, 0))
    exp_shape = jax.ShapeDtypeStruct((B, n_kv, T, LANES), BF16)
    return pl.pallas_call(
        functools.partial(_assemble_body, n_kv=n_kv, half=head_dim // 8, gpg=gpg),
        grid=(B, nt),
        in_specs=[
            pl.BlockSpec((tt, 2 * kvw), lambda b, t: (row(b, t), cb)),
            pl.BlockSpec((tt, 2 * kvw), lambda b, t: (row(b, t), cb + 1)),
            pl.BlockSpec((tt, 2 * kvw), lambda b, t: (row(b, t), cb + 2)),
            pl.BlockSpec((tt, LANES), lambda b, t: (row(b, t), gl_col // LANES)),
            pl.BlockSpec((None, 1, LANES), lambda b, t: (l, 0, 0)),
            tab, tab, tab,
        ],
        out_specs=[
            pl.BlockSpec((tt, 4 * kvw), lambda b, t: (row(b, t), 0)),
            pl.BlockSpec((tt, 2 * kvw), lambda b, t: (row(b, t), 0)),
            exp, exp, exp, exp,
            pl.BlockSpec((None, n_kv, tt, LANES), lambda b, t: (b, 0, t, 0)),
        ],
        out_shape=[
            jax.ShapeDtypeStruct((B * T, 4 * kvw), F32),
            jax.ShapeDtypeStruct((B * T, 2 * kvw), F32),
            exp_shape, exp_shape, exp_shape, exp_shape,
            jax.ShapeDtypeStruct((B, n_kv, T, LANES), F32),
        ],
        compiler_params=_cparams(("parallel", "parallel")),
        name="assemble_prompt",
    )(proj, proj, proj, proj, gate_b, *tabs)


def _compress_body(pidx_ref, src_ref, pos_ref, w1_ref, b1_ref, w2_ref, b2_ref, out_ref,
                   xbuf, acc_ref, sem, *, P, JC, NJC, G, col0, cw):
    g = pl.program_id(0)
    jc = pl.program_id(1)
    step = g * NJC + jc
    slot = lax.rem(step, 2)

    def page_copy(pg, j0, sl, p):
        return pltpu.make_async_copy(
            src_ref.at[pg, :, :, pl.ds(j0, JC), pl.ds(col0, cw)],
            xbuf.at[sl, :, :, :, p, :], sem.at[sl])

    def issue(st, sl):
        gg = st // NJC
        j0 = lax.rem(st, NJC) * JC

        def body(p, carry):
            page_copy(pidx_ref[gg * P + p], j0, sl, p).start()
            return carry
        lax.fori_loop(0, P, body, 0)

    @pl.when(step == 0)
    def _():
        issue(step, slot)

    @pl.when(step + 1 < G * NJC)
    def _():
        issue(step + 1, 1 - slot)

    def wait_body(p, carry):
        page_copy(0, 0, slot, p).wait()
        return carry
    lax.fori_loop(0, P, wait_body, 0)

    @pl.when(jc == 0)
    def _():
        acc_ref[...] = jnp.zeros(acc_ref.shape, F32)

    hw = cw // 2
    for c in range(2):
        tot = None
        for j in range(JC):
            parts = []
            for p2 in range(hw // LANES):
                col = c * hw + p2 * LANES
                for par in range(2):
                    lo = xbuf[slot, par, 0, j, :, col:col + LANES] + pos_ref[c, 0, j]
                    hi = xbuf[slot, par, 1, j, :, col:col + LANES] + pos_ref[c, 1, j]
                    parts.append(jnp.concatenate([lo, hi], axis=1))
            lhs = jnp.concatenate(parts, axis=0).astype(BF16)
            res = jnp.dot(lhs, w1_ref[c, j], preferred_element_type=F32)
            tot = res if tot is None else tot + res
        acc_ref[c] += tot

    @pl.when(jc == NJC - 1)
    def _():
        for c in range(2):
            hid = _silu(acc_ref[c] + b1_ref[c])
            comp = jnp.dot(hid.astype(BF16), w2_ref[c], preferred_element_type=F32) + b2_ref[c]
            for p2 in range(hw // LANES):
                for hl in range(2):
                    for par in range(2):
                        r0 = (p2 * 2 + par) * P
                        out_ref[c, 2 * p2 + hl, par] = (
                            comp[r0:r0 + P, hl * LANES:(hl + 1) * LANES].astype(BF16))


def _compress(page_idx, src, cmpw, l, *, G, P, col0, n_kv, head_dim):
    pos_e, w1_e, b1_e, w2_e, b2_e = cmpw
    half_rows = src.shape[3]
    JC = 8
    NJC = half_rows // JC
    cw = 2 * n_kv * head_dim
    nrow = 2 * (cw // 2 // LANES) * P
    grid_spec = pltpu.PrefetchScalarGridSpec(
        num_scalar_prefetch=1,
        grid=(G, NJC),
        in_specs=[
            pl.BlockSpec(memory_space=pl.ANY),
            pl.BlockSpec((None, 2, 2, JC, 1, LANES), lambda g, j, p: (l, 0, 0, j, 0, 0)),
            pl.BlockSpec((None, 2, JC, 2 * LANES, 2 * LANES), lambda g, j, p: (l, 0, j, 0, 0)),
            pl.BlockSpec((None, 2, 1, 2 * LANES), lambda g, j, p: (l, 0, 0, 0)),
            pl.BlockSpec((None, 2, 2 * LANES, 2 * LANES), lambda g, j, p: (l, 0, 0, 0)),
            pl.BlockSpec((None, 2, 1, 2 * LANES), lambda g, j, p: (l, 0, 0, 0)),
        ],
        out_specs=pl.BlockSpec((2, n_kv, 2, P, LANES), lambda g, j, p: (0, 0, 0, g, 0)),
        scratch_shapes=[
            pltpu.VMEM((2, 2, 2, JC, P, cw), F32),
            pltpu.VMEM((2, nrow, 2 * LANES), F32),
            pltpu.SemaphoreType.DMA((2,)),
        ],
    )
    return pl.pallas_call(
        functools.partial(_compress_body, P=P, JC=JC, NJC=NJC, G=G, col0=col0, cw=cw),
        grid_spec=grid_spec,
        out_shape=jax.ShapeDtypeStruct((2, n_kv, 2, G * P, LANES), BF16),
        compiler_params=_cparams(("arbitrary", "arbitrary")),
        name="compress",
    )(page_idx, src, pos_e, w1_e, b1_e, w2_e, b2_e)


def _compress_weights(cmp_pos, cmp_w1, cmp_b1, cmp_w2, cmp_b2):
    depth, _, blk, hd = cmp_pos.shape
    hidden = cmp_w1.shape[-1]
    hr = blk // 2
    eye = jnp.eye(2, dtype=F32)
    pos_e = jnp.tile(cmp_pos.reshape(depth, 2, 2, hr, 1, hd), (1, 1, 1, 1, 1, 2))
    w1 = cmp_w1.reshape(depth, 2, 2, hr, hd, hidden)
    w1_e = jnp.einsum("lcajde,hk->lcjahdke", w1, eye).reshape(depth, 2, hr, 4 * hd, 2 * hidden)
    b1_e = jnp.tile(cmp_b1.reshape(depth, 2, 1, hidden), (1, 1, 1, 2))
    w2_e = jnp.einsum("lced,hk,u->lchekud", cmp_w2, eye, jnp.ones((2,), F32))
    w2_e = w2_e.reshape(depth, 2, 2 * hidden, 4 * hd)
    b2_e = jnp.tile(cmp_b2.reshape(depth, 2, 1, hd), (1, 1, 1, 4))
    return pos_e, w1_e.astype(BF16), b1_e, w2_e.astype(BF16), b2_e


def _compress_cache_body(pidx_ref, src_ref, pos_ref, w1_ref, b1_ref, w2_ref, b2_ref, out_ref,
                         xbuf, acc_ref, sem, *, P, R, NRC, G, hd):
    g = pl.program_id(0)
    rc = pl.program_id(1)
    step = g * NRC + rc
    slot = lax.rem(step, 2)
    nrow = P * 2 * 8

    def page_copy(pg, r0, sl, p):
        return pltpu.make_async_copy(src_ref.at[pg, :, pl.ds(r0, R), 0], xbuf.at[sl, p], sem.at[sl])

    def issue(st, sl):
        gg = st // NRC
        r0 = lax.rem(st, NRC) * R

        def body(p, carry):
            page_copy(pidx_ref[gg * P + p], r0, sl, p).start()
            return carry
        lax.fori_loop(0, P, body, 0)

    @pl.when(step == 0)
    def _():
        issue(step, slot)

    @pl.when(step + 1 < G * NRC)
    def _():
        issue(step + 1, 1 - slot)

    def wait_body(p, carry):
        page_copy(0, 0, slot, p).wait()
        return carry
    lax.fori_loop(0, P, wait_body, 0)

    @pl.when(rc == 0)
    def _():
        acc_ref[...] = jnp.zeros(acc_ref.shape, F32)

    tot = None
    for gq in range(R // 4):
        pieces = []
        for rr in range(4):
            r = gq * 4 + rr
            pieces.append((xbuf[slot, :, :, r] + pos_ref[r]).reshape(nrow, hd))
        lhs = jnp.concatenate(pieces, axis=1).astype(BF16)
        res = jnp.dot(lhs, w1_ref[gq], preferred_element_type=F32)
        tot = res if tot is None else tot + res
    acc_ref[...] += tot

    @pl.when(rc == NRC - 1)
    def _():
        is_v = (lax.broadcasted_iota(jnp.int32, (nrow, 1), 0) & 4) != 0
        a = acc_ref[...]
        hid = jnp.where(is_v, a[:, LANES:], a[:, :LANES]) + jnp.where(is_v, b1_ref[1], b1_ref[0])
        comp = jnp.dot(_silu(hid).astype(BF16), w2_ref[...], preferred_element_type=F32)
        out_ref[...] = (jnp.where(is_v, comp[:, LANES:], comp[:, :LANES])
                        + jnp.where(is_v, b2_ref[1], b2_ref[0]))


def _compress_cache(page_idx, src, cw, l, *, G, P, head_dim):
    pos_c, w1_c, b1_c, w2_c, b2_c = cw
    blk = src.shape[2]
    R = 8
    NRC = blk // R
    nrow = P * 2 * 8
    grid_spec = pltpu.PrefetchScalarGridSpec(
        num_scalar_prefetch=1,
        grid=(G, NRC),
        in_specs=[
            pl.BlockSpec(memory_space=pl.ANY),
            pl.BlockSpec((None, R, 8, head_dim), lambda g, r, p: (l, r, 0, 0)),
            pl.BlockSpec((None, R // 4, 2 * LANES, 2 * LANES), lambda g, r, p: (l, r, 0, 0)),
            pl.BlockSpec((None, 2, 1, LANES), lambda g, r, p: (l, 0, 0, 0)),
            pl.BlockSpec((None, LANES, 2 * LANES), lambda g, r, p: (l, 0, 0)),
            pl.BlockSpec((None, 2, 1, LANES), lambda g, r, p: (l, 0, 0, 0)),
        ],
        out_specs=pl.BlockSpec((None, nrow, LANES), lambda g, r, p: (g, 0, 0)),
        scratch_shapes=[
            pltpu.VMEM((2, P, 2, R, 8, head_dim), F32),
            pltpu.VMEM((nrow, 2 * LANES), F32),
            pltpu.SemaphoreType.DMA((2,)),
        ],
    )
    return pl.pallas_call(
        functools.partial(_compress_cache_body, P=P, R=R, NRC=NRC, G=G, hd=head_dim),
        grid_spec=grid_spec,
        out_shape=jax.ShapeDtypeStruct((G, nrow, LANES), F32),
        compiler_params=_cparams(("arbitrary", "arbitrary")),
        name="compress_cache",
    )(page_idx, src, pos_c, w1_c, b1_c, w2_c, b2_c)


def _compress_cache_weights(cmp_pos, cmp_w1, cmp_b1, cmp_w2, cmp_b2, n_kv):
    depth, _, blk, hd = cmp_pos.shape
    hidden = cmp_w1.shape[-1]
    pos_c = jnp.broadcast_to(cmp_pos.transpose(0, 2, 1, 3)[:, :, :, None, :],
                             (depth, blk, 2, n_kv, hd)).reshape(depth, blk, 2 * n_kv, hd)
    w1_c = cmp_w1.reshape(depth, 2, blk // 4, 4 * hd, hidden).transpose(0, 2, 3, 1, 4)
    w1_c = w1_c.reshape(depth, blk // 4, 4 * hd, 2 * hidden).astype(BF16)
    b1_c = cmp_b1.reshape(depth, 2, 1, hidden)
    w2_c = jnp.broadcast_to(cmp_w2.transpose(0, 2, 1, 3)[:, :, :, None, :], (depth, hidden, 2, 2, hd))
    w2_c = w2_c.reshape(depth, hidden, 4 * hd).astype(BF16)
    b2_c = jnp.tile(cmp_b2.reshape(depth, 2, 1, hd), (1, 1, 1, 2))
    return pos_c, w1_c, b1_c, w2_c, b2_c


def _stack_heads(cols, lo):
    rows = []
    for cj in cols:
        rows.append(jnp.where(lo, cj, 0.0))
        rows.append(jnp.where(lo, 0.0, cj))
    return jnp.concatenate(rows, axis=0).astype(BF16)


def _block_ids(n, pb):
    return jnp.where(n >= pb, 2 * (n - pb) + 1, 2 * n)


def _nsa_prompt_body(q_ref, z_ref, gt_ref, tc_ref, ta_ref, tb_ref, ks_ref, vs_ref, kw_ref, vw_ref,
                     kc_ref, vc_ref, o_ref, *, tq, tk, T, PB, blk, half, scale):
    qi = pl.program_id(2)
    t0 = qi * tq
    nb = 2 * PB
    shift = blk.bit_length() - 1
    tc, ta, tb = tc_ref[...], ta_ref[...], tb_ref[...]
    lo = lax.broadcasted_iota(jnp.int32, (tq, LANES), 1) < LANES // 2
    q = q_ref[...] * scale
    q0, q1 = q[:, 0:LANES], q[:, LANES:2 * LANES]
    qc = _stack_heads([q0, q1], lo)
    qr = _stack_heads([_rope(q0, tc, ta, tb, half), _rope(q1, tc, ta, tb, half)], lo)
    tpos = t0 + lax.broadcasted_iota(jnp.int32, (tq, 1), 0)
    cur = lax.shift_right_logical(tpos, shift)

    def tile4(x):
        return jnp.concatenate([x, x, x, x], axis=0)

    zrow = jnp.zeros((LANES - nb, LANES), BF16)
    kc = jnp.concatenate([kc_ref[0], kc_ref[1], zrow], axis=0)
    vc = jnp.concatenate([vc_ref[0], vc_ref[1], zrow], axis=0)
    bid = _block_ids(lax.broadcasted_iota(jnp.int32, (nb, tq), 0), PB)
    tpos_l = t0 + lax.broadcasted_iota(jnp.int32, (nb, tq), 1)
    cur_l = lax.shift_right_logical(tpos_l, shift)
    cmask = (bid + 1) * blk - 1 <= tpos_l

    def lanes4(x):
        return jnp.concatenate([x, x, x, x], axis=1)

    cm4 = lanes4(cmask.astype(F32)) > 0.5
    sc = lax.dot_general(kc, qc, NT_DIMS, preferred_element_type=F32)[0:nb]
    sc = jnp.where(cm4, sc, NEG)
    mc = jnp.max(sc, axis=0, keepdims=True)
    ec = jnp.where(cm4, jnp.exp(sc - mc), 0.0)
    pct = ec / jnp.maximum(jnp.sum(ec, axis=0, keepdims=True), 1e-30)
    imp = pct[:, 0:tq] + pct[:, tq:2 * tq] + pct[:, 2 * tq:3 * tq] + pct[:, 3 * tq:4 * tq]

    forced = (bid == 0) | (bid == cur_l) | (bid == cur_l - 1)
    valid = bid <= cur_l
    score = jnp.where(valid, jnp.where(forced, BIG, imp), -BIG)
    cnt = jnp.zeros((nb, tq), F32)
    for i in range(nb):
        other = score[i:i + 1, :]
        bi = 2 * (i % PB) + i // PB
        beats = (other > score) | ((other == score) & (bid > bi))
        cnt = cnt + beats.astype(F32)
    sel_t = ((cnt < TOP_N) & valid).astype(F32)

    zpad = jnp.zeros((LANES - nb, tq), F32)
    pc = jnp.concatenate(
        [jnp.transpose(jnp.concatenate([pct[:, hh * tq:(hh + 1) * tq], zpad], axis=0))
         for hh in range(4)], axis=0)
    oc = jnp.dot(pc.astype(BF16), vc, preferred_element_type=F32)
    sel = jnp.transpose(jnp.concatenate([sel_t, zpad], axis=0)).astype(BF16)

    row_l = lax.broadcasted_iota(jnp.int32, (LANES, tk), 0)
    bid_col = jnp.where(row_l < nb, _block_ids(row_l, PB), -1)
    key_l = lax.broadcasted_iota(jnp.int32, (LANES, tk), 1)
    key_q = lax.broadcasted_iota(jnp.int32, (tq, tk), 1)

    def sel_chunk(ci, carry):
        m, l, acc = carry
        k0 = pl.multiple_of(ci * tk, tk)
        s = lax.dot_general(qr, ks_ref[pl.ds(k0, tk), :], NT_DIMS, preferred_element_type=F32)
        expand = (bid_col == lax.shift_right_logical(key_l + k0, shift)).astype(BF16)
        bm = jnp.dot(sel, expand, preferred_element_type=F32)
        ok = (bm > 0.5) & (key_q + k0 <= tpos)
        s = s + tile4(jnp.where(ok, 0.0, NEG))
        m_new = jnp.maximum(m, jnp.max(s, axis=-1, keepdims=True))
        alpha = jnp.exp(m - m_new)
        p = jnp.exp(s - m_new)
        l = alpha * l + jnp.sum(p, axis=-1, keepdims=True)
        acc = alpha * acc + jnp.dot(p.astype(BF16), vs_ref[pl.ds(k0, tk), :],
                                    preferred_element_type=F32)
        return m_new, l, acc

    nchunks = (t0 + tq + tk - 1) // tk
    init = (jnp.full((4 * tq, 1), NEG, F32), jnp.zeros((4 * tq, 1), F32),
            jnp.zeros((4 * tq, LANES), F32))
    _, ls, accs = lax.fori_loop(0, nchunks, sel_chunk, init)
    osel = accs / ls

    ww = min(WINDOW + tq, T)
    ws = pl.multiple_of(jnp.clip(t0 - WINDOW, 0, T - ww), tq)
    sw = lax.dot_general(qr, kw_ref[pl.ds(ws, ww), :], NT_DIMS, preferred_element_type=F32)
    dl = tpos - (ws + lax.broadcasted_iota(jnp.int32, (tq, ww), 1))
    sw = sw + tile4(jnp.where((dl >= 0) & (dl < WINDOW), 0.0, NEG))
    pw = jnp.exp(sw - jnp.max(sw, axis=-1, keepdims=True))
    ow = jnp.dot(pw.astype(BF16), vw_ref[pl.ds(ws, ww), :], preferred_element_type=F32)
    ow = ow / jnp.sum(pw, axis=-1, keepdims=True)

    g = gt_ref[...]

    def gcol(br):
        return jnp.concatenate([g[:, 3 * hh + br:3 * hh + br + 1] for hh in range(4)], axis=0)

    o = gcol(0) * oc + gcol(1) * osel + gcol(2) * ow
    p0 = jnp.where(lo, o[0:tq], o[tq:2 * tq])
    p1 = jnp.where(lo, o[2 * tq:3 * tq], o[3 * tq:4 * tq])
    o_ref[...] = (jnp.concatenate([p0, p1], axis=1) * _silu(z_ref[...])).astype(o_ref.dtype)


def _nsa_prompt(proj, gates, tabs, ks, vs, kw, vw, cmp_e, *, B, T, n_kv, head_dim, q_col, z_col,
                blk, tq, tk):
    nq = T // tq
    gw = 2 * LANES
    PB = T // (2 * blk)
    row = lambda b, h, i: b * nq + i
    tab = pl.BlockSpec((tq, LANES), lambda b, h, i: (i, 0))
    kvs = pl.BlockSpec((None, None, T, LANES), lambda b, h, i: (b, h, 0, 0))
    return pl.pallas_call(
        functools.partial(_nsa_prompt_body, tq=tq, tk=tk, T=T, PB=PB, blk=blk,
                          half=head_dim // 8, scale=head_dim ** -0.5),
        grid=(B, n_kv, nq),
        in_specs=[
            pl.BlockSpec((tq, gw), lambda b, h, i: (row(b, h, i), q_col // gw + h)),
            pl.BlockSpec((tq, gw), lambda b, h, i: (row(b, h, i), z_col // gw + h)),
            pl.BlockSpec((None, None, tq, LANES), lambda b, h, i: (b, h, i, 0)),
            tab, tab, tab,
            kvs, kvs, kvs, kvs,
            pl.BlockSpec((None, None, 2, PB, LANES), lambda b, h, i: (0, h, 0, b, 0)),
            pl.BlockSpec((None, None, 2, PB, LANES), lambda b, h, i: (1, h, 0, b, 0)),
        ],
        out_specs=pl.BlockSpec((tq, gw), lambda b, h, i: (row(b, h, i), h)),
        out_shape=jax.ShapeDtypeStruct((B * T, n_kv * gw), BF16),
        compiler_params=_cparams(("parallel", "parallel", "arbitrary")),
        name="nsa_prompt",
    )(proj, proj, gates, *tabs, ks, vs, kw, vw, cmp_e, cmp_e)


def _outproj_body(x_ref, a_ref, b_ref, wa_ref, wb_ref, g_ref, o_ref):
    acc = jnp.dot(a_ref[...].astype(BF16), wa_ref[...], preferred_element_type=F32)
    acc = acc + jnp.dot(b_ref[...].astype(BF16), wb_ref[...], preferred_element_type=F32)
    ms = jnp.mean(acc * acc, axis=-1, keepdims=True)
    o_ref[...] = x_ref[...] + (acc * lax.rsqrt(ms + EPS)) * g_ref[...]


def _outproj(x, a, b, w_out, g, l, *, tm):
    M, D = x.shape
    ka, kb = a.shape[1], b.shape[1]
    return pl.pallas_call(
        _outproj_body,
        grid=(M // tm,),
        in_specs=[
            pl.BlockSpec((tm, D), lambda i: (i, 0)),
            pl.BlockSpec((tm, ka), lambda i: (i, 0)),
            pl.BlockSpec((tm, kb), lambda i: (i, 0)),
            pl.BlockSpec((None, ka, D), lambda i: (l, 0, 0)),
            pl.BlockSpec((None, kb, D), lambda i: (l, ka // kb, 0)),
            pl.BlockSpec((None, 1, D), lambda i: (l, 0, 0)),
        ],
        out_specs=pl.BlockSpec((tm, D), lambda i: (i, 0)),
        out_shape=jax.ShapeDtypeStruct((M, D), F32),
        compiler_params=_cparams(("parallel",)),
        name="outproj",
    )(x, a, b, w_out, w_out, g)


def _sample_prep_body(p_ref, st_ref, w_ref, cb_ref, lg_ref, lb_ref, gb_ref,
                      tc_ref, ta_ref, tb_ref,
                      y_ref, cn_ref, kvn_ref, win_ref, qc_ref, qr_ref, gt_ref, zs_ref,
                      *, C, HD, kvw, half, scale):
    nb = p_ref.shape[0]
    taps = w_ref.shape[0]
    tc, ta, tb = tc_ref[...], ta_ref[...], tb_ref[...]
    glu = p_ref[:, 0:C] * jax.nn.sigmoid(p_ref[:, C:2 * C])
    rows = []
    for b in range(nb):
        st = st_ref[b]
        rows.append(jnp.sum(st * w_ref[0:taps - 1, :], axis=0, keepdims=True))
        cn_ref[b, 0:taps - 2, :] = st[1:taps - 1, :]
        cn_ref[b, taps - 2:taps - 1, :] = glu[b:b + 1, :]
    c = jnp.concatenate(rows, axis=0) + w_ref[taps - 1:taps, :] * glu + cb_ref[...]
    mu = jnp.mean(c, axis=-1, keepdims=True)
    d = c - mu
    var = jnp.mean(d * d, axis=-1, keepdims=True)
    yl = (d * lax.rsqrt(var + EPS)) * lg_ref[...] + lb_ref[...]
    y_ref[...] = _silu(yl) * _silu(p_ref[:, 2 * C:3 * C])

    q_col = 3 * C
    z_col = q_col + HD
    kv_col = z_col + HD
    for j in range(HD // LANES):
        qj = p_ref[:, q_col + j * LANES:q_col + (j + 1) * LANES] * scale
        qc_ref[:, j * LANES:(j + 1) * LANES] = qj
        qr_ref[:, j * LANES:(j + 1) * LANES] = _rope(qj, tc, ta, tb, half)
    zs_ref[...] = _silu(p_ref[:, z_col:z_col + HD])
    kvn_ref[:, 0:2 * kvw] = p_ref[:, kv_col:kv_col + 2 * kvw]
    kvn_ref[:, 3 * kvw:4 * kvw] = p_ref[:, kv_col + 3 * kvw:kv_col + 4 * kvw]
    win_ref[:, kvw:2 * kvw] = p_ref[:, kv_col + 5 * kvw:kv_col + 6 * kvw]
    for j in range(kvw // LANES):
        s0 = kv_col + 2 * kvw + j * LANES
        kvn_ref[:, 2 * kvw + j * LANES:2 * kvw + (j + 1) * LANES] = _rope(
            p_ref[:, s0:s0 + LANES], tc, ta, tb, half)
        w0 = kv_col + 4 * kvw + j * LANES
        win_ref[:, j * LANES:(j + 1) * LANES] = _rope(p_ref[:, w0:w0 + LANES], tc, ta, tb, half)
    gl_col = kv_col + 6 * kvw
    gt_ref[...] = jax.nn.sigmoid(p_ref[:, gl_col:gl_col + LANES] + gb_ref[...])


def _sample_prep(proj, state_conv, conv_w, conv_b, ln_g, ln_b, gate_b, tabs, l,
                 *, C, HD, kvw, head_dim):
    nb = proj.shape[0]
    taps = conv_w.shape[1]
    whole = lambda shape: pl.BlockSpec(shape, lambda i: (0,) * len(shape))
    vecc = pl.BlockSpec((None, 1, C), lambda i: (l, 0, 0))
    tab = whole((1, LANES))
    shapes = [
        ((nb, C), F32), ((nb, taps - 1, C), F32), ((nb, 4 * kvw), F32), ((nb, 2 * kvw), F32),
        ((nb, HD), F32), ((nb, HD), F32), ((nb, LANES), F32), ((nb, HD), F32),
    ]
    return pl.pallas_call(
        functools.partial(_sample_prep_body, C=C, HD=HD, kvw=kvw, half=head_dim // 8,
                          scale=head_dim ** -0.5),
        grid=(1,),
        in_specs=[
            whole(proj.shape),
            pl.BlockSpec((None, nb, taps - 1, C), lambda i: (l, 0, 0, 0)),
            pl.BlockSpec((None, taps, C), lambda i: (l, 0, 0)),
            vecc, vecc, vecc,
            pl.BlockSpec((None, 1, LANES), lambda i: (l, 0, 0)),
            tab, tab, tab,
        ],
        out_specs=[whole(s) for s, _ in shapes],
        out_shape=[jax.ShapeDtypeStruct(s, d) for s, d in shapes],
        compiler_params=_cparams(("arbitrary",)),
        name="sample_prep",
    )(proj, state_conv, conv_w, conv_b, ln_g, ln_b, gate_b, *tabs)


def _sample_cmp_body(qc_ref, cmp_ref, oc_ref, idx_ref, *, P, n_kv):
    b = pl.program_id(0)
    nb = 2 * P
    qrow = qc_ref[pl.ds(b, 1), :]
    lo = lax.broadcasted_iota(jnp.int32, (1, LANES), 1) < LANES // 2
    zero4 = jnp.zeros((4, LANES), F32)
    imps = []
    for h in range(n_kv):
        c0 = qrow[:, 2 * h * LANES:(2 * h + 1) * LANES]
        c1 = qrow[:, (2 * h + 1) * LANES:(2 * h + 2) * LANES]
        qs = jnp.concatenate([jnp.where(lo, c0, 0.0), jnp.where(lo, 0.0, c0),
                              jnp.where(lo, c1, 0.0), jnp.where(lo, 0.0, c1), zero4],
                             axis=0).astype(BF16)
        kc = cmp_ref[pl.ds(h, nb, stride=2 * n_kv), :].astype(BF16)
        vc = cmp_ref[pl.ds(n_kv + h, nb, stride=2 * n_kv), :].astype(BF16)
        s = lax.dot_general(qs, kc, NT_DIMS, preferred_element_type=F32)
        e = jnp.exp(s - jnp.max(s, axis=-1, keepdims=True))
        pc = e / jnp.maximum(jnp.sum(e, axis=-1, keepdims=True), 1e-30)
        oc_ref[h] = jnp.dot(pc.astype(BF16), vc, preferred_element_type=F32)
        imps.append(pc[0:1] + pc[1:2] + pc[2:3] + pc[3:4])
    imp = jnp.concatenate(imps, axis=0)
    bid = lax.broadcasted_iota(jnp.int32, (n_kv, nb), 1)
    forced = (bid == 0) | (bid == nb - 1)
    score = jnp.where(forced, BIG, imp)
    cnt = (score < BIG).astype(F32)
    for i in range(nb):
        col = score[:, i:i + 1]
        beats = (col > score) | ((col == score) & (bid > i))
        cnt = cnt + beats.astype(F32)
    lane = lax.broadcasted_iota(jnp.int32, (n_kv, LANES), 1)
    res = jnp.full((n_kv, LANES), -1.0, F32)
    bid1 = (bid + 1).astype(F32)
    for k in range(TOP_N):
        v = jnp.sum(jnp.where(cnt == k, bid1, 0.0), axis=-1, keepdims=True) - 1.0
        res = jnp.where(lane == k, v, res)
    idx_ref[...] = res.astype(jnp.int32)


def _sample_cmp(qc, cmp_c, *, DB, P, n_kv):
    return pl.pallas_call(
        functools.partial(_sample_cmp_body, P=P, n_kv=n_kv),
        grid=(DB,),
        in_specs=[pl.BlockSpec(qc.shape, lambda b: (0, 0)),
                  pl.BlockSpec((None,) + cmp_c.shape[1:], lambda b: (b, 0, 0))],
        out_specs=[pl.BlockSpec((None, n_kv, 8, LANES), lambda b: (b, 0, 0, 0)),
                   pl.BlockSpec((None, n_kv, LANES), lambda b: (b, 0, 0))],
        out_shape=[jax.ShapeDtypeStruct((DB, n_kv, 8, LANES), F32),
                   jax.ShapeDtypeStruct((DB, n_kv, LANES), jnp.int32)],
        compiler_params=_cparams(("parallel",)),
        name="sample_cmp",
    )(qc, cmp_c)


def _sample_attn_body(idx_ref, pages_ref, qr_ref, gt_ref, zs_ref, oc_ref, kvn_ref, wr_ref, cw_ref,
                      cache_ref, o_ref, wn_ref, selbuf, sem, *, P, n_kv, hd, kvw, nwin):
    b = pl.program_id(0)

    def blk_copy(src_blk, slot):
        return pltpu.make_async_copy(cache_ref.at[src_blk, :, 1], selbuf.at[slot], sem.at[0])

    for h in range(n_kv):
        for k in range(TOP_N):
            ix = jnp.maximum(idx_ref[(b * n_kv + h) * LANES + k], 0)
            pg = pages_ref[b * P + lax.shift_right_logical(ix, 1)]
            blk_copy(pg * 2 + (ix & 1), h * TOP_N + k).start()

    wn_ref[0:nwin - 1, :] = cw_ref[1:nwin, :]
    wn_ref[nwin - 1:nwin, :] = wr_ref[pl.ds(b, 1), :]

    for s in range(n_kv * TOP_N):
        blk_copy(0, s).wait()

    qrow = qr_ref[pl.ds(b, 1), :]
    kvrow = kvn_ref[pl.ds(b, 1), :]
    g = gt_ref[pl.ds(b, 1), :]
    zero4 = jnp.zeros((4, hd), F32)
    zero41 = jnp.zeros((4, 1), F32)
    rows = selbuf.shape[1]
    nk = TOP_N * rows
    slot_of_key = lax.broadcasted_iota(jnp.int32, (1, nk), 1) // rows
    pieces = []
    for h in range(n_kv):
        qh = jnp.concatenate([qrow[:, (4 * h + gg) * hd:(4 * h + gg + 1) * hd] for gg in range(4)]
                             + [zero4], axis=0)
        qb = qh.astype(BF16)
        ks_new = kvrow[:, 2 * kvw + h * hd:2 * kvw + (h + 1) * hd]
        vs_new = kvrow[:, 3 * kvw + h * hd:3 * kvw + (h + 1) * hd]
        kh = selbuf[h * TOP_N:(h + 1) * TOP_N, :, h, :].reshape(nk, hd).astype(BF16)
        vh = selbuf[h * TOP_N:(h + 1) * TOP_N, :, n_kv + h, :].reshape(nk, hd).astype(BF16)
        dead = jnp.zeros((1, nk), jnp.int32)
        for k in range(TOP_N):
            gone = (idx_ref[(b * n_kv + h) * LANES + k] < 0).astype(jnp.int32)
            dead = jnp.where(slot_of_key == k, gone, dead)
        s_all = lax.dot_general(qb, kh, NT_DIMS, preferred_element_type=F32)
        s_all = jnp.where(dead > 0, NEG, s_all)
        s_new = jnp.sum(qb.astype(F32) * ks_new.astype(BF16).astype(F32), axis=-1, keepdims=True)
        m = jnp.maximum(jnp.max(s_all, axis=-1, keepdims=True), s_new)
        e = jnp.exp(s_all - m)
        e_new = jnp.exp(s_new - m)
        lsum = jnp.sum(e, axis=-1, keepdims=True) + e_new
        acc = e_new.astype(BF16).astype(F32) * vs_new.astype(BF16).astype(F32)
        acc = acc + jnp.dot(e.astype(BF16), vh, preferred_element_type=F32)
        osel = acc / lsum

        kwn = wn_ref[:, h * hd:(h + 1) * hd].astype(BF16)
        vwn = wn_ref[:, kvw + h * hd:kvw + (h + 1) * hd].astype(BF16)
        sw = lax.dot_general(qb, kwn, NT_DIMS, preferred_element_type=F32)
        pw = jnp.exp(sw - jnp.max(sw, axis=-1, keepdims=True))
        ow = jnp.dot(pw.astype(BF16), vwn, preferred_element_type=F32)
        ow = ow / jnp.sum(pw, axis=-1, keepdims=True)

        def gcol(br):
            return jnp.concatenate(
                [g[:, (4 * h + gg) * 3 + br:(4 * h + gg) * 3 + br + 1] for gg in range(4)]
                + [zero41], axis=0)

        o = gcol(0) * oc_ref[h][:, 0:hd] + gcol(1) * osel + gcol(2) * ow
        for gg in range(4):
            pieces.append(o[gg:gg + 1, :])
    o_ref[pl.ds(b, 1), :] = jnp.concatenate(pieces, axis=1) * zs_ref[pl.ds(b, 1), :]


def _sample_attn(idx, pages, qr, gates, zs, oc, kvn, winrow, cache_win, cache_blk, l,
                 *, DB, P, n_kv, head_dim):
    kvw = n_kv * head_dim
    nwin = cache_win.shape[1]
    blk_rows = cache_blk.shape[1]
    whole = lambda a: pl.BlockSpec(a.shape, lambda b, i, p: (0,) * a.ndim)
    grid_spec = pltpu.PrefetchScalarGridSpec(
        num_scalar_prefetch=2,
        grid=(DB,),
        in_specs=[
            whole(qr), whole(gates), whole(zs),
            pl.BlockSpec((None, n_kv, 8, LANES), lambda b, i, p: (b, 0, 0, 0)),
            whole(kvn), whole(winrow),
            pl.BlockSpec((None, nwin, 2 * kvw), lambda b, i, p: (l * DB + b, 0, 0)),
            pl.BlockSpec(memory_space=pl.ANY),
        ],
        out_specs=[
            pl.BlockSpec(qr.shape, lambda b, i, p: (0, 0)),
            pl.BlockSpec((None, nwin, 2 * kvw), lambda b, i, p: (b, 0, 0)),
        ],
        scratch_shapes=[
            pltpu.VMEM((n_kv * TOP_N, blk_rows, 2 * n_kv, head_dim), F32),
            pltpu.SemaphoreType.DMA((1,)),
        ],
    )
    return pl.pallas_call(
        functools.partial(_sample_attn_body, P=P, n_kv=n_kv, hd=head_dim, kvw=kvw, nwin=nwin),
        grid_spec=grid_spec,
        out_shape=[jax.ShapeDtypeStruct(qr.shape, F32),
                   jax.ShapeDtypeStruct((DB, nwin, 2 * kvw), F32)],
        compiler_params=_cparams(("arbitrary",)),
        name="sample_attn",
    )(idx, pages, qr, gates, zs, oc, kvn, winrow, cache_win, cache_blk)


def kernel(x_prompt, x_sample, cache_kv_pages, cache_win, state_conv, page_table, w_in, w_out,
           norm_pre, norm_post, conv_w, conv_b, conv_ln_g, conv_ln_b, cmp_pos, cmp_w1, cmp_b1,
           cmp_w2, cmp_b2, gate_b):
    B, S, D = x_prompt.shape
    DB, dec_seq, _ = x_sample.shape
    depth, n_pool, page, _, n_kv, head_dim = cache_kv_pages.shape
    n_pages = page_table.shape[1]
    past_len = n_pages * page
    C = conv_w.shape[-1]
    taps = conv_w.shape[1]
    n_heads = gate_b.shape[-1] // 3
    HD = n_heads * head_dim
    kvw = n_kv * head_dim
    blk = cmp_pos.shape[2]
    n_main = 3 * C + 2 * HD + 6 * kvw
    nwin = cache_win.shape[2]
    gpg = 3 * (n_heads // n_kv)
    assert dec_seq == 1 and page == 2 * blk and past_len % blk == 0
    assert w_in.shape[-1] == n_main + 3 * n_heads and n_heads == 4 * n_kv and head_dim * 2 == LANES
    assert nwin == WINDOW and S >= WINDOW and past_len // blk >= TOP_N and S % page == 0

    w_out_b = w_out.astype(BF16)
    cmpw_c = _compress_cache_weights(cmp_pos, cmp_w1, cmp_b1, cmp_w2, cmp_b2, n_kv)
    gate_bp = jnp.pad(gate_b, ((0, 0), (0, LANES - 3 * n_heads))).reshape(depth, 1, LANES)
    cmpw = _compress_weights(cmp_pos, cmp_w1, cmp_b1, cmp_w2, cmp_b2)
    vec3 = lambda v: v.reshape(depth, 1, v.shape[-1])
    norm_pre3, norm_post3 = vec3(norm_pre), vec3(norm_post)
    conv_b3, ln_g3, ln_b3 = vec3(conv_b), vec3(conv_ln_g), vec3(conv_ln_b)

    tabs_p = _rope_tables(jnp.arange(S, dtype=jnp.int32), head_dim)
    tabs_s = _rope_tables(jnp.full((1,), past_len, jnp.int32), head_dim)

    cache_pg = cache_kv_pages.reshape(depth * n_pool, 2, blk, 2, 2 * n_kv, head_dim)
    cache_blk = cache_kv_pages.reshape(depth * n_pool * 2, blk, 2, 2 * n_kv, head_dim)
    cache_win2 = cache_win.reshape(depth * DB, nwin, 2 * kvw)
    pages_prompt = jnp.arange(B * S // page, dtype=jnp.int32)
    conv_zero = jnp.zeros((B, taps - 1, C), F32)

    xp = x_prompt.reshape(B * S, D)
    xs = x_sample.reshape(DB, D)
    tm_in = 1024 if (B * S) % 1024 == 0 else 256
    tm_out = 512 if (B * S) % 512 == 0 else 256
    outs = [[] for _ in range(6)]
    for l in range(depth):
        proj = _inproj(xp, norm_pre3, w_in, l, tm=tm_in, tn=512)
        yconv, conv_new = _conv_prompt(proj, conv_zero, conv_w, conv_b3, ln_g3, ln_b3, l,
                                       B=B, T=S, C=C, tt=256)
        kvn, winr, ks, vs, kw, vw, gates = _assemble_prompt(
            proj, gate_bp, tabs_p, l, B=B, T=S, n_kv=n_kv, head_dim=head_dim,
            kv_col=3 * C + 2 * HD, gl_col=n_main, tt=256, gpg=gpg)
        cmp_e = _compress(pages_prompt, proj.reshape(B * S // page, 2, 2, blk // 2, proj.shape[1]),
                          cmpw, l, G=1, P=B * S // page, col0=3 * C + 2 * HD, n_kv=n_kv,
                          head_dim=head_dim)
        attn = _nsa_prompt(proj, gates, tabs_p, ks, vs, kw, vw, cmp_e, B=B, T=S, n_kv=n_kv,
                           head_dim=head_dim, q_col=3 * C, z_col=3 * C + HD, blk=blk, tq=128, tk=512)
        xp = _outproj(xp, yconv, attn, w_out_b, norm_post3, l, tm=tm_out)
        outs[0].append(kvn.reshape(B, S, 4, n_kv, head_dim))
        outs[2].append(winr.reshape(B, S, 2, n_kv, head_dim)[:, S - nwin:])
        outs[4].append(conv_new)

        proj_s = _inproj(xs, norm_pre3, w_in, l, tm=DB, tn=512)
        (yconv_s, conv_new_s, kvn_s, winrow_s, qc_s, qr_s, gates_s, zs_s) = _sample_prep(
            proj_s, state_conv, conv_w, conv_b3, ln_g3, ln_b3, gate_bp, tabs_s, l,
            C=C, HD=HD, kvw=kvw, head_dim=head_dim)
        pages_s = (page_table + l * n_pool).reshape(-1).astype(jnp.int32)
        cmp_s = _compress_cache(pages_s, cache_pg, cmpw_c, l, G=DB, P=n_pages, head_dim=head_dim)
        oc_s, idx_s = _sample_cmp(qc_s, cmp_s, DB=DB, P=n_pages, n_kv=n_kv)
        attn_s, win_new_s = _sample_attn(idx_s.reshape(-1), pages_s, qr_s, gates_s, zs_s, oc_s, kvn_s,
                                         winrow_s, cache_win2, cache_blk, l,
                                         DB=DB, P=n_pages, n_kv=n_kv, head_dim=head_dim)
        xs = _outproj(xs, yconv_s, attn_s, w_out_b, norm_post3, l, tm=DB)
        outs[1].append(kvn_s.reshape(DB, 1, 4, n_kv, head_dim))
        outs[3].append(win_new_s.reshape(DB, nwin, 2, n_kv, head_dim))
        outs[5].append(conv_new_s)

    return (xp.reshape(B, S, D), xs.reshape(DB, 1, D), jnp.stack(outs[0]), jnp.stack(outs[1]),
            jnp.stack(outs[2]), jnp.stack(outs[3]), jnp.stack(outs[4]), jnp.stack(outs[5]))
```

```python
import functools

import jax
import jax.numpy as jnp
from jax import lax
from jax.experimental import pallas as pl
from jax.experimental.pallas import tpu as pltpu

F32 = jnp.float32
BF16 = jnp.bfloat16

TOP_N = 16
WINDOW = 512
ROPE_THETA = 500000.0
EPS = 1e-6
BIG = 1e9
NEG = -1e30
LANES = 128
VMEM_LIMIT = 56 * 1024 * 1024

NT_DIMS = (((1,), (1,)), ((), ()))


def _silu(x):
    return x * jax.nn.sigmoid(x)


def _cparams(sem):
    return pltpu.CompilerParams(dimension_semantics=sem, vmem_limit_bytes=VMEM_LIMIT)


def _inproj_body(x_ref, g_ref, w_ref, o_ref, u_ref, *, n_in, tn):
    j = pl.program_id(1)

    @pl.when(j == 0)
    def _():
        x = x_ref[...]
        ms = jnp.mean(x * x, axis=-1, keepdims=True)
        u_ref[...] = ((x * lax.rsqrt(ms + EPS)) * g_ref[...]).astype(u_ref.dtype)

    row = j * tn + lax.broadcasted_iota(jnp.int32, (tn, 1), 0)
    w = jnp.where(row < n_in, w_ref[...], 0.0).astype(BF16)
    o_ref[...] = lax.dot_general(u_ref[...].astype(BF16), w, NT_DIMS, preferred_element_type=F32)


def _inproj(x, g, w_in_t, l, *, tm, tn):
    M, D = x.shape
    n_in = w_in_t.shape[1]
    nj = pl.cdiv(n_in, tn)
    u_dtype = BF16 if tm % 16 == 0 else F32
    return pl.pallas_call(
        functools.partial(_inproj_body, n_in=n_in, tn=tn),
        grid=(M // tm, nj),
        in_specs=[
            pl.BlockSpec((tm, D), lambda i, j: (i, 0)),
            pl.BlockSpec((None, 1, D), lambda i, j: (l, 0, 0)),
            pl.BlockSpec((None, tn, D), lambda i, j: (l, j, 0)),
        ],
        out_specs=pl.BlockSpec((tm, tn), lambda i, j: (i, j)),
        out_shape=jax.ShapeDtypeStruct((M, nj * tn), F32),
        scratch_shapes=[pltpu.VMEM((tm, D), u_dtype)],
        compiler_params=_cparams(("parallel", "arbitrary")),
        name="inproj",
    )(x, g, w_in_t)


def _conv_body(av_ref, ag_ref, z_ref, buf_ref, w_ref, cb_ref, lg_ref, lb_ref,
               y_ref, cn_ref, ap_ref, zs_ref, cv_ref, *, tt, nt, taps, rc, cc):
    t = pl.program_id(1)
    hist = -(-(taps - 1) // 8) * 8
    off = hist - (taps - 1)
    C = ap_ref.shape[1]

    @pl.when(t == 0)
    def _():
        ap_ref[0:off, :] = jnp.zeros((off, C), F32)
        ap_ref[off:hist, :] = buf_ref[...]

    @pl.when(t > 0)
    def _():
        ap_ref[0:hist, :] = ap_ref[tt:tt + hist, :]

    ap_ref[hist:hist + tt, :] = av_ref[...] * jax.nn.sigmoid(ag_ref[...])

    nz = zs_ref.shape[1]
    for s in range(1, 8):
        zs_ref[s - 1] = ap_ref[s:s + nz, :]
    for r0 in range(0, tt, rc):
        for c0 in range(0, C, cc):
            acc = jnp.zeros((rc, cc), F32)
            for k in range(taps):
                s, a = (k + off) % 8, (k + off) // 8
                src = ap_ref if s == 0 else zs_ref.at[s - 1]
                acc = acc + w_ref[k:k + 1, c0:c0 + cc] * src[r0 + 8 * a:r0 + 8 * a + rc, c0:c0 + cc]
            cv_ref[r0:r0 + rc, c0:c0 + cc] = acc
        c = cv_ref[r0:r0 + rc, :] + cb_ref[...]
        mu = jnp.mean(c, axis=-1, keepdims=True)
        d = c - mu
        var = jnp.mean(d * d, axis=-1, keepdims=True)
        yl = (d * lax.rsqrt(var + EPS)) * lg_ref[...] + lb_ref[...]
        y_ref[r0:r0 + rc, :] = (_silu(yl) * _silu(z_ref[r0:r0 + rc, :])).astype(y_ref.dtype)

    @pl.when(t == nt - 1)
    def _():
        cn_ref[...] = ap_ref[tt + off:tt + hist, :]


def _conv_prompt(proj, conv_buf, conv_w, conv_b, ln_g, ln_b, l, *, B, T, C, tt):
    taps = conv_w.shape[1]
    nt = T // tt
    hist = -(-(taps - 1) // 8) * 8
    vec = pl.BlockSpec((None, 1, C), lambda b, t: (l, 0, 0))
    return pl.pallas_call(
        functools.partial(_conv_body, tt=tt, nt=nt, taps=taps, rc=32, cc=min(C, 4 * LANES)),
        grid=(B, nt),
        in_specs=[
            pl.BlockSpec((tt, C), lambda b, t: (b * nt + t, 0)),
            pl.BlockSpec((tt, C), lambda b, t: (b * nt + t, 1)),
            pl.BlockSpec((tt, C), lambda b, t: (b * nt + t, 2)),
            pl.BlockSpec((None, taps - 1, C), lambda b, t: (b, 0, 0)),
            pl.BlockSpec((None, taps, C), lambda b, t: (l, 0, 0)),
            vec, vec, vec,
        ],
        out_specs=[
            pl.BlockSpec((tt, C), lambda b, t: (b * nt + t, 0)),
            pl.BlockSpec((None, taps - 1, C), lambda b, t: (b, 0, 0)),
        ],
        out_shape=[jax.ShapeDtypeStruct((B * T, C), BF16),
                   jax.ShapeDtypeStruct((B, taps - 1, C), F32)],
        scratch_shapes=[pltpu.VMEM((tt + hist, C), F32), pltpu.VMEM((7, tt + hist - 8, C), F32),
                        pltpu.VMEM((tt, C), F32)],
        compiler_params=_cparams(("parallel", "arbitrary")),
        name="conv_prompt",
    )(proj, proj, proj, conv_buf, conv_w, conv_b, ln_g, ln_b)


def _rope_tables(pos, head_dim):
    rope_dim = head_dim // 4
    half = rope_dim // 2
    inv = ROPE_THETA ** (-jnp.arange(half, dtype=F32) / half)
    ang = pos.astype(F32)[:, None] * inv[None, :]
    cos, sin = jnp.cos(ang), jnp.sin(ang)
    d = jnp.arange(LANES) % head_dim
    f = d % half
    cosl, sinl = cos[:, f], sin[:, f]
    c = jnp.where(d < rope_dim, cosl, 1.0)
    a = jnp.where(d < half, -sinl, 0.0)
    b = jnp.where((d >= half) & (d < rope_dim), sinl, 0.0)
    return c.astype(F32), a.astype(F32), b.astype(F32)


def _rope(x, c, a, b, half):
    return x * c + pltpu.roll(x, LANES - half, 1) * a + pltpu.roll(x, half, 1) * b


def _dup_head(x, odd, lo):
    r = pltpu.roll(x, LANES // 2, 1)
    return jnp.where(lo, r, x) if odd else jnp.where(lo, x, r)


def _assemble_body(a_ref, b_ref, c_ref, gl_ref, gb_ref, tc_ref, ta_ref, tb_ref,
                   kvn_ref, win_ref, ks_ref, vs_ref, kw_ref, vw_ref, gt_ref, *, n_kv, half, gpg):
    tc, ta, tb = tc_ref[...], ta_ref[...], tb_ref[...]
    tt = tc.shape[0]
    lo = lax.broadcasted_iota(jnp.int32, (tt, LANES), 1) < LANES // 2
    kvw = n_kv * LANES // 2
    ncol = kvw // LANES

    a = a_ref[...]
    kvn_ref[:, 0:2 * kvw] = a
    b = b_ref[...]
    c = c_ref[...]
    ksr, kwr = [], []
    for j in range(ncol):
        ksr.append(_rope(b[:, j * LANES:(j + 1) * LANES], tc, ta, tb, half))
        kwr.append(_rope(c[:, j * LANES:(j + 1) * LANES], tc, ta, tb, half))
        kvn_ref[:, 2 * kvw + j * LANES:2 * kvw + (j + 1) * LANES] = ksr[j]
        win_ref[:, j * LANES:(j + 1) * LANES] = kwr[j]
    kvn_ref[:, 3 * kvw:4 * kvw] = b[:, kvw:2 * kvw]
    win_ref[:, kvw:2 * kvw] = c[:, kvw:2 * kvw]

    g = jax.nn.sigmoid(gl_ref[...] + gb_ref[...])
    for h in range(n_kv):
        j, odd = h // 2, h % 2
        ks_ref[h] = _dup_head(ksr[j], odd, lo).astype(BF16)
        vs_ref[h] = _dup_head(b[:, kvw + j * LANES:kvw + (j + 1) * LANES], odd, lo).astype(BF16)
        kw_ref[h] = _dup_head(kwr[j], odd, lo).astype(BF16)
        vw_ref[h] = _dup_head(c[:, kvw + j * LANES:kvw + (j + 1) * LANES], odd, lo).astype(BF16)
        gt_ref[h] = g if h == 0 else pltpu.roll(g, LANES - gpg * h, 1)


def _assemble_prompt(proj, gate_b, tabs, l, *, B, T, n_kv, head_dim, kv_col, gl_col, tt, gpg):
    kvw = n_kv * head_dim
    nt = T // tt
    cb = kv_col // (2 * kvw)
    row = lambda b, t: b * nt + t
    tab = pl.BlockSpec((tt, LANES), lambda b, t: (t, 0))
    exp = pl.BlockSpec((None, n_kv, tt, LANES), lambda b, t: (b, 0, t, 0))
    exp_shape = jax.ShapeDtypeStruct((B, n_kv, T, LANES), BF16)
    return pl.pallas_call(
        functools.partial(_assemble_body, n_kv=n_kv, half=head_dim // 8, gpg=gpg),
        grid=(B, nt),
        in_specs=[
            pl.BlockSpec((tt, 2 * kvw), lambda b, t: (row(b, t), cb)),
            pl.BlockSpec((tt, 2 * kvw), lambda b, t: (row(b, t), cb + 1)),
            pl.BlockSpec((tt, 2 * kvw), lambda b, t: (row(b, t), cb + 2)),
            pl.BlockSpec((tt, LANES), lambda b, t: (row(b, t), gl_col // LANES)),
            pl.BlockSpec((None, 1, LANES), lambda b, t: (l, 0, 0)),
            tab, tab, tab,
        ],
        out_specs=[
            pl.BlockSpec((tt, 4 * kvw), lambda b, t: (row(b, t), 0)),
            pl.BlockSpec((tt, 2 * kvw), lambda b, t: (row(b, t), 0)),
            exp, exp, exp, exp,
            pl.BlockSpec((None, n_kv, tt, LANES), lambda b, t: (b, 0, t, 0)),
        ],
        out_shape=[
            jax.ShapeDtypeStruct((B * T, 4 * kvw), F32),
            jax.ShapeDtypeStruct((B * T, 2 * kvw), F32),
            exp_shape, exp_shape, exp_shape, exp_shape,
            jax.ShapeDtypeStruct((B, n_kv, T, LANES), F32),
        ],
        compiler_params=_cparams(("parallel", "parallel")),
        name="assemble_prompt",
    )(proj, proj, proj, proj, gate_b, *tabs)


def _compress_body(pidx_ref, src_ref, pos_ref, w1_ref, b1_ref, w2_ref, b2_ref, out_ref,
                   xbuf, acc_ref, sem, *, P, JC, NJC, G, col0, cw):
    g = pl.program_id(0)
    jc = pl.program_id(1)
    step = g * NJC + jc
    slot = lax.rem(step, 2)

    def page_copy(pg, j0, sl, p):
        return pltpu.make_async_copy(
            src_ref.at[pg, :, :, pl.ds(j0, JC), pl.ds(col0, cw)],
            xbuf.at[sl, :, :, :, p, :], sem.at[sl])

    def issue(st, sl):
        gg = st // NJC
        j0 = lax.rem(st, NJC) * JC

        def body(p, carry):
            page_copy(pidx_ref[gg * P + p], j0, sl, p).start()
            return carry
        lax.fori_loop(0, P, body, 0)

    @pl.when(step == 0)
    def _():
        issue(step, slot)

    @pl.when(step + 1 < G * NJC)
    def _():
        issue(step + 1, 1 - slot)

    def wait_body(p, carry):
        page_copy(0, 0, slot, p).wait()
        return carry
    lax.fori_loop(0, P, wait_body, 0)

    @pl.when(jc == 0)
    def _():
        acc_ref[...] = jnp.zeros(acc_ref.shape, F32)

    hw = cw // 2
    for c in range(2):
        tot = None
        for j in range(JC):
            parts = []
            for p2 in range(hw // LANES):
                col = c * hw + p2 * LANES
                for par in range(2):
                    lo = xbuf[slot, par, 0, j, :, col:col + LANES] + pos_ref[c, 0, j]
                    hi = xbuf[slot, par, 1, j, :, col:col + LANES] + pos_ref[c, 1, j]
                    parts.append(jnp.concatenate([lo, hi], axis=1))
            lhs = jnp.concatenate(parts, axis=0).astype(BF16)
            res = jnp.dot(lhs, w1_ref[c, j], preferred_element_type=F32)
            tot = res if tot is None else tot + res
        acc_ref[c] += tot

    @pl.when(jc == NJC - 1)
    def _():
        for c in range(2):
            hid = _silu(acc_ref[c] + b1_ref[c])
            comp = jnp.dot(hid.astype(BF16), w2_ref[c], preferred_element_type=F32) + b2_ref[c]
            for p2 in range(hw // LANES):
                for hl in range(2):
                    for par in range(2):
                        r0 = (p2 * 2 + par) * P
                        out_ref[c, 2 * p2 + hl, par] = (
                            comp[r0:r0 + P, hl * LANES:(hl + 1) * LANES].astype(BF16))


def _compress(page_idx, src, cmpw, l, *, G, P, col0, n_kv, head_dim):
    pos_e, w1_e, b1_e, w2_e, b2_e = cmpw
    half_rows = src.shape[3]
    JC = 8
    NJC = half_rows // JC
    cw = 2 * n_kv * head_dim
    nrow = 2 * (cw // 2 // LANES) * P
    grid_spec = pltpu.PrefetchScalarGridSpec(
        num_scalar_prefetch=1,
        grid=(G, NJC),
        in_specs=[
            pl.BlockSpec(memory_space=pl.ANY),
            pl.BlockSpec((None, 2, 2, JC, 1, LANES), lambda g, j, p: (l, 0, 0, j, 0, 0)),
            pl.BlockSpec((None, 2, JC, 2 * LANES, 2 * LANES), lambda g, j, p: (l, 0, j, 0, 0)),
            pl.BlockSpec((None, 2, 1, 2 * LANES), lambda g, j, p: (l, 0, 0, 0)),
            pl.BlockSpec((None, 2, 2 * LANES, 2 * LANES), lambda g, j, p: (l, 0, 0, 0)),
            pl.BlockSpec((None, 2, 1, 2 * LANES), lambda g, j, p: (l, 0, 0, 0)),
        ],
        out_specs=pl.BlockSpec((2, n_kv, 2, P, LANES), lambda g, j, p: (0, 0, 0, g, 0)),
        scratch_shapes=[
            pltpu.VMEM((2, 2, 2, JC, P, cw), F32),
            pltpu.VMEM((2, nrow, 2 * LANES), F32),
            pltpu.SemaphoreType.DMA((2,)),
        ],
    )
    return pl.pallas_call(
        functools.partial(_compress_body, P=P, JC=JC, NJC=NJC, G=G, col0=col0, cw=cw),
        grid_spec=grid_spec,
        out_shape=jax.ShapeDtypeStruct((2, n_kv, 2, G * P, LANES), BF16),
        compiler_params=_cparams(("arbitrary", "arbitrary")),
        name="compress",
    )(page_idx, src, pos_e, w1_e, b1_e, w2_e, b2_e)


def _compress_weights(cmp_pos, cmp_w1, cmp_b1, cmp_w2, cmp_b2):
    depth, _, blk, hd = cmp_pos.shape
    hidden = cmp_w1.shape[-1]
    hr = blk // 2
    eye = jnp.eye(2, dtype=F32)
    pos_e = jnp.tile(cmp_pos.reshape(depth, 2, 2, hr, 1, hd), (1, 1, 1, 1, 1, 2))
    w1 = cmp_w1.reshape(depth, 2, 2, hr, hd, hidden)
    w1_e = jnp.einsum("lcajde,hk->lcjahdke", w1, eye).reshape(depth, 2, hr, 4 * hd, 2 * hidden)
    b1_e = jnp.tile(cmp_b1.reshape(depth, 2, 1, hidden), (1, 1, 1, 2))
    w2_e = jnp.einsum("lced,hk,u->lchekud", cmp_w2, eye, jnp.ones((2,), F32))
    w2_e = w2_e.reshape(depth, 2, 2 * hidden, 4 * hd)
    b2_e = jnp.tile(cmp_b2.reshape(depth, 2, 1, hd), (1, 1, 1, 4))
    return pos_e, w1_e.astype(BF16), b1_e, w2_e.astype(BF16), b2_e


def _compress_cache_body(pidx_ref, src_ref, pos_ref, w1_ref, b1_ref, w2_ref, b2_ref, out_ref,
                         xbuf, acc_ref, sem, *, P, JC, NJC, G, n_kv):
    g = pl.program_id(0)
    jc = pl.program_id(1)
    step = g * NJC + jc
    slot = lax.rem(step, 2)
    nrow = n_kv * P

    def page_copy(pg, j0, sl, p):
        return pltpu.make_async_copy(src_ref.at[pg, pl.ds(0, 2), :, :, pl.ds(j0, JC)],
                                     xbuf.at[sl, :, :, :, :, p], sem.at[sl])

    def issue(st, sl):
        gg = st // NJC
        j0 = lax.rem(st, NJC) * JC

        def body(p, carry):
            page_copy(pidx_ref[gg * P + p], j0, sl, p).start()
            return carry
        lax.fori_loop(0, P, body, 0)

    @pl.when(step == 0)
    def _():
        issue(step, slot)

    @pl.when(step + 1 < G * NJC)
    def _():
        issue(step + 1, 1 - slot)

    def wait_body(p, carry):
        page_copy(0, 0, slot, p).wait()
        return carry
    lax.fori_loop(0, P, wait_body, 0)

    @pl.when(jc == 0)
    def _():
        acc_ref[...] = jnp.zeros(acc_ref.shape, F32)

    for c in range(2):
        tot = None
        for j in range(JC):
            lo = (xbuf[slot, c, :, 0, j] + pos_ref[c, 0, j]).reshape(nrow, LANES)
            hi = (xbuf[slot, c, :, 1, j] + pos_ref[c, 1, j]).reshape(nrow, LANES)
            lhs = jnp.concatenate([lo, hi], axis=1).astype(BF16)
            res = jnp.dot(lhs, w1_ref[c, j], preferred_element_type=F32)
            tot = res if tot is None else tot + res
        acc_ref[c] += tot

    @pl.when(jc == NJC - 1)
    def _():
        for c in range(2):
            hid = _silu(acc_ref[c] + b1_ref[c])
            comp = jnp.dot(hid.astype(BF16), w2_ref[c], preferred_element_type=F32) + b2_ref[c]
            for h in range(n_kv):
                for par in range(2):
                    out_ref[c, h, par] = comp[h * P:(h + 1) * P,
                                              par * LANES:(par + 1) * LANES].astype(BF16)


def _compress_cache(page_idx, src, cw, l, *, G, P):
    pos_t, w1_t, b1_e, w2_e, b2_e = cw
    n_kv, hh, page = src.shape[2], src.shape[4], src.shape[5]
    JC = 8
    NJC = hh // JC
    grid_spec = pltpu.PrefetchScalarGridSpec(
        num_scalar_prefetch=1,
        grid=(G, NJC),
        in_specs=[
            pl.BlockSpec(memory_space=pl.ANY),
            pl.BlockSpec((None, 2, 2, JC, 1, page), lambda g, j, p: (l, 0, 0, j, 0, 0)),
            pl.BlockSpec((None, 2, JC, 2 * page, 2 * LANES), lambda g, j, p: (l, 0, j, 0, 0)),
            pl.BlockSpec((None, 2, 1, 2 * LANES), lambda g, j, p: (l, 0, 0, 0)),
            pl.BlockSpec((None, 2, 2 * LANES, 2 * LANES), lambda g, j, p: (l, 0, 0, 0)),
            pl.BlockSpec((None, 2, 1, 2 * LANES), lambda g, j, p: (l, 0, 0, 0)),
        ],
        out_specs=pl.BlockSpec((2, n_kv, 2, P, LANES), lambda g, j, p: (0, 0, 0, g, 0)),
        scratch_shapes=[
            pltpu.VMEM((2, 2, n_kv, 2, JC, P, page), F32),
            pltpu.VMEM((2, n_kv * P, 2 * LANES), F32),
            pltpu.SemaphoreType.DMA((2,)),
        ],
    )
    return pl.pallas_call(
        functools.partial(_compress_cache_body, P=P, JC=JC, NJC=NJC, G=G, n_kv=n_kv),
        grid_spec=grid_spec,
        out_shape=jax.ShapeDtypeStruct((2, n_kv, 2, G * P, LANES), BF16),
        compiler_params=_cparams(("arbitrary", "arbitrary")),
        name="compress_cache",
    )(page_idx, src, pos_t, w1_t, b1_e, w2_e, b2_e)


def _compress_cache_weights(cmp_pos, cmp_w1, cmpw):
    depth, _, blk, hd = cmp_pos.shape
    hidden = cmp_w1.shape[-1]
    hh = hd // 2
    eye = jnp.eye(2, dtype=F32)
    pos_t = jnp.tile(cmp_pos.reshape(depth, 2, blk, 2, hh).transpose(0, 1, 3, 4, 2)[:, :, :, :, None, :],
                     (1, 1, 1, 1, 1, 2))
    w1 = cmp_w1.reshape(depth, 2, blk, 2, hh, hidden)
    w1_t = jnp.einsum("lcraje,bk->lcjabrke", w1, eye).reshape(depth, 2, hh, 4 * blk, 2 * hidden)
    _, _, b1_e, w2_e, b2_e = cmpw
    return pos_t, w1_t.astype(BF16), b1_e, w2_e, b2_e


def _stack_heads(cols, lo):
    rows = []
    for cj in cols:
        rows.append(jnp.where(lo, cj, 0.0))
        rows.append(jnp.where(lo, 0.0, cj))
    return jnp.concatenate(rows, axis=0).astype(BF16)


def _block_ids(n, pb):
    return jnp.where(n >= pb, 2 * (n - pb) + 1, 2 * n)


def _nsa_prompt_body(q_ref, z_ref, gt_ref, tc_ref, ta_ref, tb_ref, ks_ref, vs_ref, kw_ref, vw_ref,
                     kc_ref, vc_ref, o_ref, *, tq, tk, T, PB, blk, half, scale):
    qi = pl.program_id(2)
    t0 = qi * tq
    nb = 2 * PB
    shift = blk.bit_length() - 1
    tc, ta, tb = tc_ref[...], ta_ref[...], tb_ref[...]
    lo = lax.broadcasted_iota(jnp.int32, (tq, LANES), 1) < LANES // 2
    q = q_ref[...] * scale
    q0, q1 = q[:, 0:LANES], q[:, LANES:2 * LANES]
    qc = _stack_heads([q0, q1], lo)
    qr = _stack_heads([_rope(q0, tc, ta, tb, half), _rope(q1, tc, ta, tb, half)], lo)
    tpos = t0 + lax.broadcasted_iota(jnp.int32, (tq, 1), 0)
    cur = lax.shift_right_logical(tpos, shift)

    def tile4(x):
        return jnp.concatenate([x, x, x, x], axis=0)

    zrow = jnp.zeros((LANES - nb, LANES), BF16)
    kc = jnp.concatenate([kc_ref[0], kc_ref[1], zrow], axis=0)
    vc = jnp.concatenate([vc_ref[0], vc_ref[1], zrow], axis=0)
    bid = _block_ids(lax.broadcasted_iota(jnp.int32, (nb, tq), 0), PB)
    tpos_l = t0 + lax.broadcasted_iota(jnp.int32, (nb, tq), 1)
    cur_l = lax.shift_right_logical(tpos_l, shift)
    cmask = (bid + 1) * blk - 1 <= tpos_l

    def lanes4(x):
        return jnp.concatenate([x, x, x, x], axis=1)

    cm4 = lanes4(cmask.astype(F32)) > 0.5
    sc = lax.dot_general(kc, qc, NT_DIMS, preferred_element_type=F32)[0:nb]
    sc = jnp.where(cm4, sc, NEG)
    mc = jnp.max(sc, axis=0, keepdims=True)
    ec = jnp.where(cm4, jnp.exp(sc - mc), 0.0)
    pct = ec / jnp.maximum(jnp.sum(ec, axis=0, keepdims=True), 1e-30)
    imp = pct[:, 0:tq] + pct[:, tq:2 * tq] + pct[:, 2 * tq:3 * tq] + pct[:, 3 * tq:4 * tq]

    forced = (bid == 0) | (bid == cur_l) | (bid == cur_l - 1)
    valid = bid <= cur_l
    score = jnp.where(valid, jnp.where(forced, BIG, imp), -BIG)
    cnt = jnp.zeros((nb, tq), F32)
    for i in range(nb):
        other = score[i:i + 1, :]
        bi = 2 * (i % PB) + i // PB
        beats = (other > score) | ((other == score) & (bid > bi))
        cnt = cnt + beats.astype(F32)
    sel_t = ((cnt < TOP_N) & valid).astype(F32)

    zpad = jnp.zeros((LANES - nb, tq), F32)
    pc = jnp.concatenate(
        [jnp.transpose(jnp.concatenate([pct[:, hh * tq:(hh + 1) * tq], zpad], axis=0))
         for hh in range(4)], axis=0)
    oc = jnp.dot(pc.astype(BF16), vc, preferred_element_type=F32)
    sel = jnp.transpose(jnp.concatenate([sel_t, zpad], axis=0)).astype(BF16)

    row_l = lax.broadcasted_iota(jnp.int32, (LANES, tk), 0)
    bid_col = jnp.where(row_l < nb, _block_ids(row_l, PB), -1)
    key_l = lax.broadcasted_iota(jnp.int32, (LANES, tk), 1)
    key_q = lax.broadcasted_iota(jnp.int32, (tq, tk), 1)

    def sel_chunk(ci, carry):
        m, l, acc = carry
        k0 = pl.multiple_of(ci * tk, tk)
        s = lax.dot_general(qr, ks_ref[pl.ds(k0, tk), :], NT_DIMS, preferred_element_type=F32)
        expand = (bid_col == lax.shift_right_logical(key_l + k0, shift)).astype(BF16)
        bm = jnp.dot(sel, expand, preferred_element_type=F32)
        ok = (bm > 0.5) & (key_q + k0 <= tpos)
        s = s + tile4(jnp.where(ok, 0.0, NEG))
        m_new = jnp.maximum(m, jnp.max(s, axis=-1, keepdims=True))
        alpha = jnp.exp(m - m_new)
        p = jnp.exp(s - m_new)
        l = alpha * l + jnp.sum(p, axis=-1, keepdims=True)
        acc = alpha * acc + jnp.dot(p.astype(BF16), vs_ref[pl.ds(k0, tk), :],
                                    preferred_element_type=F32)
        return m_new, l, acc

    nchunks = (t0 + tq + tk - 1) // tk
    init = (jnp.full((4 * tq, 1), NEG, F32), jnp.zeros((4 * tq, 1), F32),
            jnp.zeros((4 * tq, LANES), F32))
    _, ls, accs = lax.fori_loop(0, nchunks, sel_chunk, init)
    osel = accs / ls

    ww = min(WINDOW + tq, T)
    ws = pl.multiple_of(jnp.clip(t0 - WINDOW, 0, T - ww), tq)
    sw = lax.dot_general(qr, kw_ref[pl.ds(ws, ww), :], NT_DIMS, preferred_element_type=F32)
    dl = tpos - (ws + lax.broadcasted_iota(jnp.int32, (tq, ww), 1))
    sw = sw + tile4(jnp.where((dl >= 0) & (dl < WINDOW), 0.0, NEG))
    pw = jnp.exp(sw - jnp.max(sw, axis=-1, keepdims=True))
    ow = jnp.dot(pw.astype(BF16), vw_ref[pl.ds(ws, ww), :], preferred_element_type=F32)
    ow = ow / jnp.sum(pw, axis=-1, keepdims=True)

    g = gt_ref[...]

    def gcol(br):
        return jnp.concatenate([g[:, 3 * hh + br:3 * hh + br + 1] for hh in range(4)], axis=0)

    o = gcol(0) * oc + gcol(1) * osel + gcol(2) * ow
    p0 = jnp.where(lo, o[0:tq], o[tq:2 * tq])
    p1 = jnp.where(lo, o[2 * tq:3 * tq], o[3 * tq:4 * tq])
    o_ref[...] = (jnp.concatenate([p0, p1], axis=1) * _silu(z_ref[...])).astype(o_ref.dtype)


def _nsa_prompt(proj, gates, tabs, ks, vs, kw, vw, cmp_e, *, B, T, n_kv, head_dim, q_col, z_col,
                blk, tq, tk):
    nq = T // tq
    gw = 2 * LANES
    PB = T // (2 * blk)
    row = lambda b, h, i: b * nq + i
    tab = pl.BlockSpec((tq, LANES), lambda b, h, i: (i, 0))
    kvs = pl.BlockSpec((None, None, T, LANES), lambda b, h, i: (b, h, 0, 0))
    return pl.pallas_call(
        functools.partial(_nsa_prompt_body, tq=tq, tk=tk, T=T, PB=PB, blk=blk,
                          half=head_dim // 8, scale=head_dim ** -0.5),
        grid=(B, n_kv, nq),
        in_specs=[
            pl.BlockSpec((tq, gw), lambda b, h, i: (row(b, h, i), q_col // gw + h)),
            pl.BlockSpec((tq, gw), lambda b, h, i: (row(b, h, i), z_col // gw + h)),
            pl.BlockSpec((None, None, tq, LANES), lambda b, h, i: (b, h, i, 0)),
            tab, tab, tab,
            kvs, kvs, kvs, kvs,
            pl.BlockSpec((None, None, 2, PB, LANES), lambda b, h, i: (0, h, 0, b, 0)),
            pl.BlockSpec((None, None, 2, PB, LANES), lambda b, h, i: (1, h, 0, b, 0)),
        ],
        out_specs=pl.BlockSpec((tq, gw), lambda b, h, i: (row(b, h, i), h)),
        out_shape=jax.ShapeDtypeStruct((B * T, n_kv * gw), BF16),
        compiler_params=_cparams(("parallel", "parallel", "arbitrary")),
        name="nsa_prompt",
    )(proj, proj, gates, *tabs, ks, vs, kw, vw, cmp_e, cmp_e)


def _outproj_body(x_ref, a_ref, b_ref, wa_ref, wb_ref, g_ref, o_ref):
    acc = jnp.dot(a_ref[...].astype(BF16), wa_ref[...], preferred_element_type=F32)
    acc = acc + jnp.dot(b_ref[...].astype(BF16), wb_ref[...], preferred_element_type=F32)
    ms = jnp.mean(acc * acc, axis=-1, keepdims=True)
    o_ref[...] = x_ref[...] + (acc * lax.rsqrt(ms + EPS)) * g_ref[...]


def _outproj(x, a, b, w_out, g, l, *, tm):
    M, D = x.shape
    ka, kb = a.shape[1], b.shape[1]
    return pl.pallas_call(
        _outproj_body,
        grid=(M // tm,),
        in_specs=[
            pl.BlockSpec((tm, D), lambda i: (i, 0)),
            pl.BlockSpec((tm, ka), lambda i: (i, 0)),
            pl.BlockSpec((tm, kb), lambda i: (i, 0)),
            pl.BlockSpec((None, ka, D), lambda i: (l, 0, 0)),
            pl.BlockSpec((None, kb, D), lambda i: (l, ka // kb, 0)),
            pl.BlockSpec((None, 1, D), lambda i: (l, 0, 0)),
        ],
        out_specs=pl.BlockSpec((tm, D), lambda i: (i, 0)),
        out_shape=jax.ShapeDtypeStruct((M, D), F32),
        compiler_params=_cparams(("parallel",)),
        name="outproj",
    )(x, a, b, w_out, w_out, g)


def _sample_prep_body(p_ref, st_ref, w_ref, cb_ref, lg_ref, lb_ref, gb_ref,
                      tc_ref, ta_ref, tb_ref,
                      y_ref, cn_ref, kvn_ref, win_ref, qc_ref, qr_ref, gt_ref, zs_ref,
                      *, C, HD, kvw, half, scale):
    nb = p_ref.shape[0]
    taps = w_ref.shape[0]
    tc, ta, tb = tc_ref[...], ta_ref[...], tb_ref[...]
    glu = p_ref[:, 0:C] * jax.nn.sigmoid(p_ref[:, C:2 * C])
    c = w_ref[taps - 1:taps, :] * glu + cb_ref[...]
    for k in range(taps - 1):
        c = c + w_ref[k:k + 1, :] * st_ref[k]
    cn_ref[0:taps - 2] = st_ref[1:taps - 1]
    cn_ref[taps - 2] = glu
    mu = jnp.mean(c, axis=-1, keepdims=True)
    d = c - mu
    var = jnp.mean(d * d, axis=-1, keepdims=True)
    yl = (d * lax.rsqrt(var + EPS)) * lg_ref[...] + lb_ref[...]
    y_ref[...] = _silu(yl) * _silu(p_ref[:, 2 * C:3 * C])

    q_col = 3 * C
    z_col = q_col + HD
    kv_col = z_col + HD
    for j in range(HD // LANES):
        qj = p_ref[:, q_col + j * LANES:q_col + (j + 1) * LANES] * scale
        qc_ref[:, j * LANES:(j + 1) * LANES] = qj
        qr_ref[:, j * LANES:(j + 1) * LANES] = _rope(qj, tc, ta, tb, half)
    zs_ref[...] = _silu(p_ref[:, z_col:z_col + HD])
    kvn_ref[:, 0:2 * kvw] = p_ref[:, kv_col:kv_col + 2 * kvw]
    kvn_ref[:, 3 * kvw:4 * kvw] = p_ref[:, kv_col + 3 * kvw:kv_col + 4 * kvw]
    win_ref[:, kvw:2 * kvw] = p_ref[:, kv_col + 5 * kvw:kv_col + 6 * kvw]
    for j in range(kvw // LANES):
        s0 = kv_col + 2 * kvw + j * LANES
        kvn_ref[:, 2 * kvw + j * LANES:2 * kvw + (j + 1) * LANES] = _rope(
            p_ref[:, s0:s0 + LANES], tc, ta, tb, half)
        w0 = kv_col + 4 * kvw + j * LANES
        win_ref[:, j * LANES:(j + 1) * LANES] = _rope(p_ref[:, w0:w0 + LANES], tc, ta, tb, half)
    gl_col = kv_col + 6 * kvw
    gt_ref[...] = jax.nn.sigmoid(p_ref[:, gl_col:gl_col + LANES] + gb_ref[...])


def _sample_prep(proj, state_conv, conv_w, conv_b, ln_g, ln_b, gate_b, tabs, l,
                 *, C, HD, kvw, head_dim):
    nb = proj.shape[0]
    taps = conv_w.shape[1]
    whole = lambda shape: pl.BlockSpec(shape, lambda i: (0,) * len(shape))
    vecc = pl.BlockSpec((None, 1, C), lambda i: (l, 0, 0))
    tab = whole((1, LANES))
    shapes = [
        ((nb, C), F32), ((taps - 1, nb, C), F32), ((nb, 4 * kvw), F32), ((nb, 2 * kvw), F32),
        ((nb, HD), F32), ((nb, HD), F32), ((nb, LANES), F32), ((nb, HD), F32),
    ]
    return pl.pallas_call(
        functools.partial(_sample_prep_body, C=C, HD=HD, kvw=kvw, half=head_dim // 8,
                          scale=head_dim ** -0.5),
        grid=(1,),
        in_specs=[
            whole(proj.shape),
            pl.BlockSpec((None, taps - 1, nb, C), lambda i: (l, 0, 0, 0)),
            pl.BlockSpec((None, taps, C), lambda i: (l, 0, 0)),
            vecc, vecc, vecc,
            pl.BlockSpec((None, 1, LANES), lambda i: (l, 0, 0)),
            tab, tab, tab,
        ],
        out_specs=[whole(s) for s, _ in shapes],
        out_shape=[jax.ShapeDtypeStruct(s, d) for s, d in shapes],
        compiler_params=_cparams(("arbitrary",)),
        name="sample_prep",
    )(proj, state_conv, conv_w, conv_b, ln_g, ln_b, gate_b, *tabs)


def _sample_cmp_body(qc_ref, kc_ref, vc_ref, oc_ref, idx_ref, *, P, n_kv):
    b = pl.program_id(0)
    nb = 2 * P
    qrow = qc_ref[pl.ds(b, 1), :]
    lo = lax.broadcasted_iota(jnp.int32, (1, LANES), 1) < LANES // 2
    zero4 = jnp.zeros((4, LANES), F32)
    imps = []
    for h in range(n_kv):
        c0 = qrow[:, 2 * h * LANES:(2 * h + 1) * LANES]
        c1 = qrow[:, (2 * h + 1) * LANES:(2 * h + 2) * LANES]
        qs = jnp.concatenate([jnp.where(lo, c0, 0.0), jnp.where(lo, 0.0, c0),
                              jnp.where(lo, c1, 0.0), jnp.where(lo, 0.0, c1), zero4],
                             axis=0).astype(BF16)
        kc = jnp.concatenate([kc_ref[h, 0], kc_ref[h, 1]], axis=0)
        vc = jnp.concatenate([vc_ref[h, 0], vc_ref[h, 1]], axis=0)
        s = lax.dot_general(qs, kc, NT_DIMS, preferred_element_type=F32)
        e = jnp.exp(s - jnp.max(s, axis=-1, keepdims=True))
        pc = e / jnp.maximum(jnp.sum(e, axis=-1, keepdims=True), 1e-30)
        oc_ref[h] = jnp.dot(pc.astype(BF16), vc, preferred_element_type=F32)
        imps.append(pc[0:1] + pc[1:2] + pc[2:3] + pc[3:4])
    imp = jnp.concatenate(imps, axis=0)
    bid = _block_ids(lax.broadcasted_iota(jnp.int32, (n_kv, nb), 1), P)
    forced = (bid == 0) | (bid == nb - 1)
    score = jnp.where(forced, BIG, imp)
    cnt = (score < BIG).astype(F32)
    for i in range(nb):
        col = score[:, i:i + 1]
        beats = (col > score) | ((col == score) & (bid > 2 * (i % P) + i // P))
        cnt = cnt + beats.astype(F32)
    lane = lax.broadcasted_iota(jnp.int32, (n_kv, LANES), 1)
    res = jnp.full((n_kv, LANES), -1.0, F32)
    bid1 = (bid + 1).astype(F32)
    for k in range(TOP_N):
        v = jnp.sum(jnp.where(cnt == k, bid1, 0.0), axis=-1, keepdims=True) - 1.0
        res = jnp.where(lane == k, v, res)
    idx_ref[...] = res.astype(jnp.int32)


def _sample_cmp(qc, cmp_e, *, DB, P, n_kv):
    ce = lambda c: pl.BlockSpec((None, n_kv, 2, P, LANES), lambda b: (c, 0, 0, b, 0))
    return pl.pallas_call(
        functools.partial(_sample_cmp_body, P=P, n_kv=n_kv),
        grid=(DB,),
        in_specs=[pl.BlockSpec(qc.shape, lambda b: (0, 0)), ce(0), ce(1)],
        out_specs=[pl.BlockSpec((None, n_kv, 8, LANES), lambda b: (b, 0, 0, 0)),
                   pl.BlockSpec((None, n_kv, LANES), lambda b: (b, 0, 0))],
        out_shape=[jax.ShapeDtypeStruct((DB, n_kv, 8, LANES), F32),
                   jax.ShapeDtypeStruct((DB, n_kv, LANES), jnp.int32)],
        compiler_params=_cparams(("parallel",)),
        name="sample_cmp",
    )(qc, cmp_e, cmp_e)


def _sample_attn_body(idx_ref, pages_ref, qr_ref, gt_ref, zs_ref, oc_ref, kvn_ref, wr_ref, cw_ref,
                      cache_ref, o_ref, wn_ref, selbuf, sem, *, P, n_kv, hd, kvw, nwin):
    b = pl.program_id(0)

    def page_copy(pg, h, slot):
        return pltpu.make_async_copy(cache_ref.at[pg, pl.ds(2, 2), h], selbuf.at[slot], sem.at[0])

    for h in range(n_kv):
        for k in range(TOP_N):
            ix = jnp.maximum(idx_ref[(b * n_kv + h) * LANES + k], 0)
            page_copy(pages_ref[b * P + lax.shift_right_logical(ix, 1)], h, h * TOP_N + k).start()

    newrow = wr_ref[pl.ds(b, 1), :]
    eye = (lax.broadcasted_iota(jnp.int32, (hd, hd), 0)
           == lax.broadcasted_iota(jnp.int32, (hd, hd), 1))
    last = lax.broadcasted_iota(jnp.int32, (hd, nwin), 1) == nwin - 1
    for c in range(2):
        for h in range(n_kv):
            v = newrow[:, c * kvw + h * hd:c * kvw + (h + 1) * hd]
            col = jnp.sum(jnp.where(eye, v, 0.0), axis=1, keepdims=True)
            wn_ref[c, h] = jnp.where(last, col, pltpu.roll(cw_ref[c, h], nwin - 1, 1))

    for s in range(n_kv * TOP_N):
        page_copy(0, 0, s).wait()

    qrow = qr_ref[pl.ds(b, 1), :]
    kvrow = kvn_ref[pl.ds(b, 1), :]
    g = gt_ref[pl.ds(b, 1), :]
    zero4 = jnp.zeros((4, hd), F32)
    zero41 = jnp.zeros((4, 1), F32)
    page = selbuf.shape[3]
    rows = page // 2
    nk = TOP_N * page
    key = lax.broadcasted_iota(jnp.int32, (1, nk), 1)
    slot_of_key = key // page
    half_of_key = (key // rows) & 1
    pieces = []
    for h in range(n_kv):
        qh = jnp.concatenate([qrow[:, (4 * h + gg) * hd:(4 * h + gg + 1) * hd] for gg in range(4)]
                             + [zero4], axis=0)
        qb = qh.astype(BF16)
        ks_new = kvrow[:, 2 * kvw + h * hd:2 * kvw + (h + 1) * hd]
        vs_new = kvrow[:, 3 * kvw + h * hd:3 * kvw + (h + 1) * hd]
        kt = jnp.concatenate([selbuf[h * TOP_N + k, 0] for k in range(TOP_N)], axis=1).astype(BF16)
        vt = jnp.concatenate([selbuf[h * TOP_N + k, 1] for k in range(TOP_N)], axis=1).astype(BF16)
        want = jnp.full((1, nk), -1, jnp.int32)
        for k in range(TOP_N):
            ix = idx_ref[(b * n_kv + h) * LANES + k]
            want = jnp.where(slot_of_key == k, jnp.where(ix >= 0, ix & 1, -1), want)
        s_all = jnp.dot(qb, kt, preferred_element_type=F32)
        s_all = jnp.where(half_of_key == want, s_all, NEG)
        s_new = jnp.sum(qb.astype(F32) * ks_new.astype(BF16).astype(F32), axis=-1, keepdims=True)
        m = jnp.maximum(jnp.max(s_all, axis=-1, keepdims=True), s_new)
        e = jnp.exp(s_all - m)
        e_new = jnp.exp(s_new - m)
        lsum = jnp.sum(e, axis=-1, keepdims=True) + e_new
        acc = e_new.astype(BF16).astype(F32) * vs_new.astype(BF16).astype(F32)
        acc = acc + lax.dot_general(e.astype(BF16), vt, NT_DIMS, preferred_element_type=F32)
        osel = acc / lsum

        sw = jnp.dot(qb, wn_ref[0, h].astype(BF16), preferred_element_type=F32)
        pw = jnp.exp(sw - jnp.max(sw, axis=-1, keepdims=True))
        ow = lax.dot_general(pw.astype(BF16), wn_ref[1, h].astype(BF16), NT_DIMS,
                             preferred_element_type=F32)
        ow = ow / jnp.sum(pw, axis=-1, keepdims=True)

        def gcol(br):
            return jnp.concatenate(
                [g[:, (4 * h + gg) * 3 + br:(4 * h + gg) * 3 + br + 1] for gg in range(4)]
                + [zero41], axis=0)

        o = gcol(0) * oc_ref[h][:, 0:hd] + gcol(1) * osel + gcol(2) * ow
        for gg in range(4):
            pieces.append(o[gg:gg + 1, :])
    o_ref[pl.ds(b, 1), :] = jnp.concatenate(pieces, axis=1) * zs_ref[pl.ds(b, 1), :]


def _sample_attn(idx, pages, qr, gates, zs, oc, kvn, winrow, cache_win_t, cache_t, l,
                 *, DB, P, n_kv, head_dim):
    kvw = n_kv * head_dim
    nwin = cache_win_t.shape[-1]
    page = cache_t.shape[-1]
    win_blk = (None, 2, n_kv, head_dim, nwin)
    whole = lambda a: pl.BlockSpec(a.shape, lambda b, i, p: (0,) * a.ndim)
    grid_spec = pltpu.PrefetchScalarGridSpec(
        num_scalar_prefetch=2,
        grid=(DB,),
        in_specs=[
            whole(qr), whole(gates), whole(zs),
            pl.BlockSpec((None, n_kv, 8, LANES), lambda b, i, p: (b, 0, 0, 0)),
            whole(kvn), whole(winrow),
            pl.BlockSpec(win_blk, lambda b, i, p: (l * DB + b, 0, 0, 0, 0)),
            pl.BlockSpec(memory_space=pl.ANY),
        ],
        out_specs=[
            pl.BlockSpec(qr.shape, lambda b, i, p: (0, 0)),
            pl.BlockSpec(win_blk, lambda b, i, p: (b, 0, 0, 0, 0)),
        ],
        scratch_shapes=[
            pltpu.VMEM((n_kv * TOP_N, 2, head_dim, page), F32),
            pltpu.SemaphoreType.DMA((1,)),
        ],
    )
    return pl.pallas_call(
        functools.partial(_sample_attn_body, P=P, n_kv=n_kv, hd=head_dim, kvw=kvw, nwin=nwin),
        grid_spec=grid_spec,
        out_shape=[jax.ShapeDtypeStruct(qr.shape, F32),
                   jax.ShapeDtypeStruct((DB, 2, n_kv, head_dim, nwin), F32)],
        compiler_params=_cparams(("arbitrary",)),
        name="sample_attn",
    )(idx, pages, qr, gates, zs, oc, kvn, winrow, cache_win_t, cache_t)


def kernel(x_prompt, x_sample, cache_kv_pages, cache_win, state_conv, page_table, w_in, w_out,
           norm_pre, norm_post, conv_w, conv_b, conv_ln_g, conv_ln_b, cmp_pos, cmp_w1, cmp_b1,
           cmp_w2, cmp_b2, gate_b):
    B, S, D = x_prompt.shape
    DB, dec_seq, _ = x_sample.shape
    depth, n_pool, page, _, n_kv, head_dim = cache_kv_pages.shape
    n_pages = page_table.shape[1]
    past_len = n_pages * page
    C = conv_w.shape[-1]
    taps = conv_w.shape[1]
    n_heads = gate_b.shape[-1] // 3
    HD = n_heads * head_dim
    kvw = n_kv * head_dim
    blk = cmp_pos.shape[2]
    n_main = 3 * C + 2 * HD + 6 * kvw
    nwin = cache_win.shape[2]
    gpg = 3 * (n_heads // n_kv)
    assert dec_seq == 1 and page == 2 * blk and past_len % blk == 0
    assert w_in.shape[-1] == n_main + 3 * n_heads and n_heads == 4 * n_kv and head_dim * 2 == LANES
    assert nwin == WINDOW and S >= WINDOW and past_len // blk >= TOP_N and S % page == 0

    w_out_b = w_out.astype(BF16)
    gate_bp = jnp.pad(gate_b, ((0, 0), (0, LANES - 3 * n_heads))).reshape(depth, 1, LANES)
    cmpw = _compress_weights(cmp_pos, cmp_w1, cmp_b1, cmp_w2, cmp_b2)
    cmpw_c = _compress_cache_weights(cmp_pos, cmp_w1, cmpw)
    vec3 = lambda v: v.reshape(depth, 1, v.shape[-1])
    norm_pre3, norm_post3 = vec3(norm_pre), vec3(norm_post)
    conv_b3, ln_g3, ln_b3 = vec3(conv_b), vec3(conv_ln_g), vec3(conv_ln_b)

    tabs_p = _rope_tables(jnp.arange(S, dtype=jnp.int32), head_dim)
    tabs_s = _rope_tables(jnp.full((1,), past_len, jnp.int32), head_dim)

    cache_t = cache_kv_pages.transpose(0, 1, 3, 4, 5, 2).reshape(depth * n_pool, 4, n_kv, head_dim, page)
    cache_t6 = cache_t.reshape(depth * n_pool, 4, n_kv, 2, head_dim // 2, page)
    cache_win_t = cache_win.transpose(0, 1, 3, 4, 5, 2).reshape(depth * DB, 2, n_kv, head_dim, nwin)
    state_t = state_conv.transpose(0, 2, 1, 3)
    w_in_t = w_in.transpose(0, 2, 1)
    pages_prompt = jnp.arange(B * S // page, dtype=jnp.int32)
    conv_zero = jnp.zeros((B, taps - 1, C), F32)

    xp = x_prompt.reshape(B * S, D)
    xs = x_sample.reshape(DB, D)
    tm_in = 1024 if (B * S) % 1024 == 0 else 256
    tm_out = 512 if (B * S) % 512 == 0 else 256
    outs = [[] for _ in range(6)]
    for l in range(depth):
        proj = _inproj(xp, norm_pre3, w_in_t, l, tm=tm_in, tn=512)
        yconv, conv_new = _conv_prompt(proj, conv_zero, conv_w, conv_b3, ln_g3, ln_b3, l,
                                       B=B, T=S, C=C, tt=256)
        kvn, winr, ks, vs, kw, vw, gates = _assemble_prompt(
            proj, gate_bp, tabs_p, l, B=B, T=S, n_kv=n_kv, head_dim=head_dim,
            kv_col=3 * C + 2 * HD, gl_col=n_main, tt=256, gpg=gpg)
        cmp_e = _compress(pages_prompt, proj.reshape(B * S // page, 2, 2, blk // 2, proj.shape[1]),
                          cmpw, l, G=1, P=B * S // page, col0=3 * C + 2 * HD, n_kv=n_kv,
                          head_dim=head_dim)
        attn = _nsa_prompt(proj, gates, tabs_p, ks, vs, kw, vw, cmp_e, B=B, T=S, n_kv=n_kv,
                           head_dim=head_dim, q_col=3 * C, z_col=3 * C + HD, blk=blk, tq=128, tk=512)
        xp = _outproj(xp, yconv, attn, w_out_b, norm_post3, l, tm=tm_out)
        outs[0].append(kvn.reshape(B, S, 4, n_kv, head_dim))
        outs[2].append(winr.reshape(B, S, 2, n_kv, head_dim)[:, S - nwin:])
        outs[4].append(conv_new)

        proj_s = _inproj(xs, norm_pre3, w_in_t, l, tm=DB, tn=512)
        (yconv_s, conv_new_s, kvn_s, winrow_s, qc_s, qr_s, gates_s, zs_s) = _sample_prep(
            proj_s, state_t, conv_w, conv_b3, ln_g3, ln_b3, gate_bp, tabs_s, l,
            C=C, HD=HD, kvw=kvw, head_dim=head_dim)
        pages_s = (page_table + l * n_pool).reshape(-1).astype(jnp.int32)
        cmp_s = _compress_cache(pages_s, cache_t6, cmpw_c, l, G=DB, P=n_pages)
        oc_s, idx_s = _sample_cmp(qc_s, cmp_s, DB=DB, P=n_pages, n_kv=n_kv)
        attn_s, win_new_s = _sample_attn(idx_s.reshape(-1), pages_s, qr_s, gates_s, zs_s, oc_s, kvn_s,
                                         winrow_s, cache_win_t, cache_t, l,
                                         DB=DB, P=n_pages, n_kv=n_kv, head_dim=head_dim)
        xs = _outproj(xs, yconv_s, attn_s, w_out_b, norm_post3, l, tm=DB)
        outs[1].append(kvn_s.reshape(DB, 1, 4, n_kv, head_dim))
        outs[3].append(win_new_s.transpose(0, 4, 1, 2, 3))
        outs[5].append(conv_new_s.transpose(1, 0, 2))

    return (xp.reshape(B, S, D), xs.reshape(DB, 1, D), jnp.stack(outs[0]), jnp.stack(outs[1]),
            jnp.stack(outs[2]), jnp.stack(outs[3]), jnp.stack(outs[4]), jnp.stack(outs[5]))
```

```python
import functools

import jax
import jax.numpy as jnp
from jax import lax
from jax.experimental import pallas as pl
from jax.experimental.pallas import tpu as pltpu

F32 = jnp.float32
BF16 = jnp.bfloat16

TOP_N = 16
WINDOW = 512
ROPE_THETA = 500000.0
EPS = 1e-6
BIG = 1e9
NEG = -1e30
LANES = 128
VMEM_LIMIT = 56 * 1024 * 1024

NT_DIMS = (((1,), (1,)), ((), ()))


def _silu(x):
    return x * jax.nn.sigmoid(x)


def _cparams(sem):
    return pltpu.CompilerParams(dimension_semantics=sem, vmem_limit_bytes=VMEM_LIMIT)


def _inproj_body(x_ref, g_ref, w_ref, o_ref, u_ref, *, n_in, tn):
    j = pl.program_id(1)

    @pl.when(j == 0)
    def _():
        x = x_ref[...]
        ms = jnp.mean(x * x, axis=-1, keepdims=True)
        u_ref[...] = ((x * lax.rsqrt(ms + EPS)) * g_ref[...]).astype(u_ref.dtype)

    nj = pl.num_programs(1)

    @pl.when(j < nj - 1)
    def _():
        o_ref[...] = lax.dot_general(u_ref[...].astype(BF16), w_ref[...].astype(BF16), NT_DIMS,
                                     preferred_element_type=F32)

    @pl.when(j == nj - 1)
    def _():
        row = j * tn + lax.broadcasted_iota(jnp.int32, (tn, 1), 0)
        w = jnp.where(row < n_in, w_ref[...], 0.0).astype(BF16)
        o_ref[...] = lax.dot_general(u_ref[...].astype(BF16), w, NT_DIMS, preferred_element_type=F32)


def _inproj(x, g, w_in_t, l, *, tm, tn):
    M, D = x.shape
    n_in = w_in_t.shape[1]
    nj = pl.cdiv(n_in, tn)
    u_dtype = BF16 if tm % 16 == 0 else F32
    return pl.pallas_call(
        functools.partial(_inproj_body, n_in=n_in, tn=tn),
        grid=(M // tm, nj),
        in_specs=[
            pl.BlockSpec((tm, D), lambda i, j: (i, 0)),
            pl.BlockSpec((None, 1, D), lambda i, j: (l, 0, 0)),
            pl.BlockSpec((None, tn, D), lambda i, j: (l, j, 0)),
        ],
        out_specs=pl.BlockSpec((tm, tn), lambda i, j: (i, j)),
        out_shape=jax.ShapeDtypeStruct((M, nj * tn), F32),
        scratch_shapes=[pltpu.VMEM((tm, D), u_dtype)],
        compiler_params=_cparams(("parallel", "arbitrary")),
        name="inproj",
    )(x, g, w_in_t)


def _conv_body(av_ref, ag_ref, z_ref, buf_ref, w_ref, cb_ref, lg_ref, lb_ref,
               y_ref, cn_ref, ap_ref, zs_ref, cv_ref, *, tt, nt, taps, rc, cc):
    t = pl.program_id(1)
    hist = -(-(taps - 1) // 8) * 8
    off = hist - (taps - 1)
    C = ap_ref.shape[1]

    @pl.when(t == 0)
    def _():
        ap_ref[0:off, :] = jnp.zeros((off, C), F32)
        ap_ref[off:hist, :] = buf_ref[...]

    @pl.when(t > 0)
    def _():
        ap_ref[0:hist, :] = ap_ref[tt:tt + hist, :]

    ap_ref[hist:hist + tt, :] = av_ref[...] * jax.nn.sigmoid(ag_ref[...])

    nz = zs_ref.shape[1]
    for s in range(1, 8):
        zs_ref[s - 1] = ap_ref[s:s + nz, :]
    for r0 in range(0, tt, rc):
        for c0 in range(0, C, cc):
            acc = jnp.zeros((rc, cc), F32)
            for k in range(taps):
                s, a = (k + off) % 8, (k + off) // 8
                src = ap_ref if s == 0 else zs_ref.at[s - 1]
                acc = acc + w_ref[k:k + 1, c0:c0 + cc] * src[r0 + 8 * a:r0 + 8 * a + rc, c0:c0 + cc]
            cv_ref[r0:r0 + rc, c0:c0 + cc] = acc
        c = cv_ref[r0:r0 + rc, :] + cb_ref[...]
        mu = jnp.mean(c, axis=-1, keepdims=True)
        d = c - mu
        var = jnp.mean(d * d, axis=-1, keepdims=True)
        yl = (d * lax.rsqrt(var + EPS)) * lg_ref[...] + lb_ref[...]
        y_ref[r0:r0 + rc, :] = (_silu(yl) * _silu(z_ref[r0:r0 + rc, :])).astype(y_ref.dtype)

    @pl.when(t == nt - 1)
    def _():
        cn_ref[...] = ap_ref[tt + off:tt + hist, :]


def _conv_prompt(proj, conv_buf, conv_w, conv_b, ln_g, ln_b, l, *, B, T, C, tt):
    taps = conv_w.shape[1]
    nt = T // tt
    hist = -(-(taps - 1) // 8) * 8
    vec = pl.BlockSpec((None, 1, C), lambda b, t: (l, 0, 0))
    return pl.pallas_call(
        functools.partial(_conv_body, tt=tt, nt=nt, taps=taps, rc=32, cc=min(C, 4 * LANES)),
        grid=(B, nt),
        in_specs=[
            pl.BlockSpec((tt, C), lambda b, t: (b * nt + t, 0)),
            pl.BlockSpec((tt, C), lambda b, t: (b * nt + t, 1)),
            pl.BlockSpec((tt, C), lambda b, t: (b * nt + t, 2)),
            pl.BlockSpec((None, taps - 1, C), lambda b, t: (b, 0, 0)),
            pl.BlockSpec((None, taps, C), lambda b, t: (l, 0, 0)),
            vec, vec, vec,
        ],
        out_specs=[
            pl.BlockSpec((tt, C), lambda b, t: (b * nt + t, 0)),
            pl.BlockSpec((None, taps - 1, C), lambda b, t: (b, 0, 0)),
        ],
        out_shape=[jax.ShapeDtypeStruct((B * T, C), BF16),
                   jax.ShapeDtypeStruct((B, taps - 1, C), F32)],
        scratch_shapes=[pltpu.VMEM((tt + hist, C), F32), pltpu.VMEM((7, tt + hist - 8, C), F32),
                        pltpu.VMEM((tt, C), F32)],
        compiler_params=_cparams(("parallel", "arbitrary")),
        name="conv_prompt",
    )(proj, proj, proj, conv_buf, conv_w, conv_b, ln_g, ln_b)


def _rope_tables(pos, head_dim):
    rope_dim = head_dim // 4
    half = rope_dim // 2
    inv = ROPE_THETA ** (-jnp.arange(half, dtype=F32) / half)
    ang = pos.astype(F32)[:, None] * inv[None, :]
    cos, sin = jnp.cos(ang), jnp.sin(ang)
    d = jnp.arange(LANES) % head_dim
    f = d % half
    cosl, sinl = cos[:, f], sin[:, f]
    c = jnp.where(d < rope_dim, cosl, 1.0)
    a = jnp.where(d < half, -sinl, 0.0)
    b = jnp.where((d >= half) & (d < rope_dim), sinl, 0.0)
    return c.astype(F32), a.astype(F32), b.astype(F32)


def _rope(x, c, a, b, half):
    return x * c + pltpu.roll(x, LANES - half, 1) * a + pltpu.roll(x, half, 1) * b


def _dup_head(x, odd, lo):
    r = pltpu.roll(x, LANES // 2, 1)
    return jnp.where(lo, r, x) if odd else jnp.where(lo, x, r)


def _assemble_body(a_ref, b_ref, c_ref, gl_ref, gb_ref, tc_ref, ta_ref, tb_ref,
                   kvn_ref, win_ref, ks_ref, vs_ref, kw_ref, vw_ref, gt_ref, *, n_kv, half, gpg):
    tc, ta, tb = tc_ref[...], ta_ref[...], tb_ref[...]
    tt = tc.shape[0]
    lo = lax.broadcasted_iota(jnp.int32, (tt, LANES), 1) < LANES // 2
    kvw = n_kv * LANES // 2
    ncol = kvw // LANES

    hd = LANES // 2

    def put_t(ref, kind, j, x):
        xt = jnp.transpose(x)
        ref[kind, 2 * j] = xt[0:hd]
        ref[kind, 2 * j + 1] = xt[hd:2 * hd]

    a = a_ref[...]
    b = b_ref[...]
    c = c_ref[...]
    ksr, kwr = [], []
    for j in range(ncol):
        ksr.append(_rope(b[:, j * LANES:(j + 1) * LANES], tc, ta, tb, half))
        kwr.append(_rope(c[:, j * LANES:(j + 1) * LANES], tc, ta, tb, half))
        put_t(kvn_ref, 0, j, a[:, j * LANES:(j + 1) * LANES])
        put_t(kvn_ref, 1, j, a[:, kvw + j * LANES:kvw + (j + 1) * LANES])
        put_t(kvn_ref, 2, j, ksr[j])
        put_t(kvn_ref, 3, j, b[:, kvw + j * LANES:kvw + (j + 1) * LANES])
        put_t(win_ref, 0, j, kwr[j])
        put_t(win_ref, 1, j, c[:, kvw + j * LANES:kvw + (j + 1) * LANES])

    g = jax.nn.sigmoid(gl_ref[...] + gb_ref[...])
    for h in range(n_kv):
        j, odd = h // 2, h % 2
        ks_ref[h] = _dup_head(ksr[j], odd, lo).astype(BF16)
        vs_ref[h] = _dup_head(b[:, kvw + j * LANES:kvw + (j + 1) * LANES], odd, lo).astype(BF16)
        kw_ref[h] = _dup_head(kwr[j], odd, lo).astype(BF16)
        vw_ref[h] = _dup_head(c[:, kvw + j * LANES:kvw + (j + 1) * LANES], odd, lo).astype(BF16)
        gt_ref[h] = g if h == 0 else pltpu.roll(g, LANES - gpg * h, 1)


def _assemble_prompt(proj, gate_b, tabs, l, *, B, T, n_kv, head_dim, kv_col, gl_col, tt, gpg, nwin):
    kvw = n_kv * head_dim
    assert nwin % tt == 0 and T >= nwin
    nt = T // tt
    cb = kv_col // (2 * kvw)
    row = lambda b, t: b * nt + t
    tab = pl.BlockSpec((tt, LANES), lambda b, t: (t, 0))
    exp = pl.BlockSpec((None, n_kv, tt, LANES), lambda b, t: (b, 0, t, 0))
    exp_shape = jax.ShapeDtypeStruct((B, n_kv, T, LANES), BF16)
    return pl.pallas_call(
        functools.partial(_assemble_body, n_kv=n_kv, half=head_dim // 8, gpg=gpg),
        grid=(B, nt),
        in_specs=[
            pl.BlockSpec((tt, 2 * kvw), lambda b, t: (row(b, t), cb)),
            pl.BlockSpec((tt, 2 * kvw), lambda b, t: (row(b, t), cb + 1)),
            pl.BlockSpec((tt, 2 * kvw), lambda b, t: (row(b, t), cb + 2)),
            pl.BlockSpec((tt, LANES), lambda b, t: (row(b, t), gl_col // LANES)),
            pl.BlockSpec((None, 1, LANES), lambda b, t: (l, 0, 0)),
            tab, tab, tab,
        ],
        out_specs=[
            pl.BlockSpec((None, 4, n_kv, head_dim, tt), lambda b, t: (b, 0, 0, 0, t)),
            pl.BlockSpec((None, 2, n_kv, head_dim, tt),
                         lambda b, t: (b, 0, 0, 0, jnp.maximum(t - (nt - nwin // tt), 0))),
            exp, exp, exp, exp,
            pl.BlockSpec((None, n_kv, tt, LANES), lambda b, t: (b, 0, t, 0)),
        ],
        out_shape=[
            jax.ShapeDtypeStruct((B, 4, n_kv, head_dim, T), F32),
            jax.ShapeDtypeStruct((B, 2, n_kv, head_dim, nwin), F32),
            exp_shape, exp_shape, exp_shape, exp_shape,
            jax.ShapeDtypeStruct((B, n_kv, T, LANES), F32),
        ],
        compiler_params=_cparams(("parallel", "arbitrary")),
        name="assemble_prompt",
    )(proj, proj, proj, proj, gate_b, *tabs)


def _compress_body(pidx_ref, src_ref, pos_ref, w1_ref, b1_ref, w2_ref, b2_ref, out_ref,
                   xbuf, acc_ref, sem, *, P, JC, NJC, G, col0, cw):
    g = pl.program_id(0)
    jc = pl.program_id(1)
    step = g * NJC + jc
    slot = lax.rem(step, 2)

    def page_copy(pg, j0, sl, p):
        return pltpu.make_async_copy(
            src_ref.at[pg, :, :, pl.ds(j0, JC), pl.ds(col0, cw)],
            xbuf.at[sl, :, :, :, p, :], sem.at[sl])

    def issue(st, sl):
        gg = st // NJC
        j0 = lax.rem(st, NJC) * JC

        def body(p, carry):
            page_copy(pidx_ref[gg * P + p], j0, sl, p).start()
            return carry
        lax.fori_loop(0, P, body, 0)

    @pl.when(step == 0)
    def _():
        issue(step, slot)

    @pl.when(step + 1 < G * NJC)
    def _():
        issue(step + 1, 1 - slot)

    def wait_body(p, carry):
        page_copy(0, 0, slot, p).wait()
        return carry
    lax.fori_loop(0, P, wait_body, 0)

    @pl.when(jc == 0)
    def _():
        acc_ref[...] = jnp.zeros(acc_ref.shape, F32)

    hw = cw // 2
    for c in range(2):
        tot = None
        for j in range(JC):
            parts = []
            for p2 in range(hw // LANES):
                col = c * hw + p2 * LANES
                for par in range(2):
                    lo = xbuf[slot, par, 0, j, :, col:col + LANES] + pos_ref[c, 0, j]
                    hi = xbuf[slot, par, 1, j, :, col:col + LANES] + pos_ref[c, 1, j]
                    parts.append(jnp.concatenate([lo, hi], axis=1))
            lhs = jnp.concatenate(parts, axis=0).astype(BF16)
            res = jnp.dot(lhs, w1_ref[c, j], preferred_element_type=F32)
            tot = res if tot is None else tot + res
        acc_ref[c] += tot

    @pl.when(jc == NJC - 1)
    def _():
        for c in range(2):
            hid = _silu(acc_ref[c] + b1_ref[c])
            comp = jnp.dot(hid.astype(BF16), w2_ref[c], preferred_element_type=F32) + b2_ref[c]
            for p2 in range(hw // LANES):
                for hl in range(2):
                    for par in range(2):
                        r0 = (p2 * 2 + par) * P
                        out_ref[c, 2 * p2 + hl, par] = (
                            comp[r0:r0 + P, hl * LANES:(hl + 1) * LANES].astype(BF16))


def _compress(page_idx, src, cmpw, l, *, G, P, col0, n_kv, head_dim):
    pos_e, w1_e, b1_e, w2_e, b2_e = cmpw
    half_rows = src.shape[3]
    JC = 8
    NJC = half_rows // JC
    cw = 2 * n_kv * head_dim
    nrow = 2 * (cw // 2 // LANES) * P
    grid_spec = pltpu.PrefetchScalarGridSpec(
        num_scalar_prefetch=1,
        grid=(G, NJC),
        in_specs=[
            pl.BlockSpec(memory_space=pl.ANY),
            pl.BlockSpec((None, 2, 2, JC, 1, LANES), lambda g, j, p: (l, 0, 0, j, 0, 0)),
            pl.BlockSpec((None, 2, JC, 2 * LANES, 2 * LANES), lambda g, j, p: (l, 0, j, 0, 0)),
            pl.BlockSpec((None, 2, 1, 2 * LANES), lambda g, j, p: (l, 0, 0, 0)),
            pl.BlockSpec((None, 2, 2 * LANES, 2 * LANES), lambda g, j, p: (l, 0, 0, 0)),
            pl.BlockSpec((None, 2, 1, 2 * LANES), lambda g, j, p: (l, 0, 0, 0)),
        ],
        out_specs=pl.BlockSpec((2, n_kv, 2, P, LANES), lambda g, j, p: (0, 0, 0, g, 0)),
        scratch_shapes=[
            pltpu.VMEM((2, 2, 2, JC, P, cw), F32),
            pltpu.VMEM((2, nrow, 2 * LANES), F32),
            pltpu.SemaphoreType.DMA((2,)),
        ],
    )
    return pl.pallas_call(
        functools.partial(_compress_body, P=P, JC=JC, NJC=NJC, G=G, col0=col0, cw=cw),
        grid_spec=grid_spec,
        out_shape=jax.ShapeDtypeStruct((2, n_kv, 2, G * P, LANES), BF16),
        compiler_params=_cparams(("arbitrary", "arbitrary")),
        name="compress",
    )(page_idx, src, pos_e, w1_e, b1_e, w2_e, b2_e)


def _blockdiag2(base):
    z = jnp.zeros_like(base)
    return jnp.concatenate([jnp.concatenate([base, z], axis=-1),
                            jnp.concatenate([z, base], axis=-1)], axis=-2)


def _compress_weights(cmp_pos, cmp_w1, cmp_b1, cmp_w2, cmp_b2):
    depth, _, blk, hd = cmp_pos.shape
    hidden = cmp_w1.shape[-1]
    hr = blk // 2
    pos_e = jnp.tile(cmp_pos.reshape(depth, 2, 2, hr, 1, hd), (1, 1, 1, 1, 1, 2))
    w1 = cmp_w1.reshape(depth, 2, 2, hr, hd, hidden).transpose(0, 1, 3, 2, 4, 5)
    w1_e = _blockdiag2(w1).reshape(depth, 2, hr, 4 * hd, 2 * hidden)
    b1_e = jnp.tile(cmp_b1.reshape(depth, 2, 1, hidden), (1, 1, 1, 2))
    w2_e = _blockdiag2(jnp.concatenate([cmp_w2, cmp_w2], axis=-1))
    b2_e = jnp.tile(cmp_b2.reshape(depth, 2, 1, hd), (1, 1, 1, 4))
    return pos_e, w1_e.astype(BF16), b1_e, w2_e.astype(BF16), b2_e


def _compress_cache_body(pidx_ref, src_ref, pos_ref, w1_ref, b1_ref, w2_ref, b2_ref, out_ref,
                         xbuf, acc_ref, sem, *, P, JC, NJC, G, n_kv):
    g = pl.program_id(0)
    jc = pl.program_id(1)
    step = g * NJC + jc
    slot = lax.rem(step, 2)
    nrow = n_kv * P

    def page_copy(pg, j0, sl, p):
        return pltpu.make_async_copy(src_ref.at[pg, pl.ds(0, 2), :, :, pl.ds(j0, JC)],
                                     xbuf.at[sl, :, :, :, :, p], sem.at[sl])

    def issue(st, sl):
        gg = st // NJC
        j0 = lax.rem(st, NJC) * JC

        def body(p, carry):
            page_copy(pidx_ref[gg * P + p], j0, sl, p).start()
            return carry
        lax.fori_loop(0, P, body, 0)

    @pl.when(step == 0)
    def _():
        issue(step, slot)

    @pl.when(step + 1 < G * NJC)
    def _():
        issue(step + 1, 1 - slot)

    def wait_body(p, carry):
        page_copy(0, 0, slot, p).wait()
        return carry
    lax.fori_loop(0, P, wait_body, 0)

    @pl.when(jc == 0)
    def _():
        acc_ref[...] = jnp.zeros(acc_ref.shape, F32)

    for c in range(2):
        tot = None
        for j in range(JC):
            lo = (xbuf[slot, c, :, 0, j] + pos_ref[c, 0, j]).reshape(nrow, LANES)
            hi = (xbuf[slot, c, :, 1, j] + pos_ref[c, 1, j]).reshape(nrow, LANES)
            lhs = jnp.concatenate([lo, hi], axis=1).astype(BF16)
            res = jnp.dot(lhs, w1_ref[c, j], preferred_element_type=F32)
            tot = res if tot is None else tot + res
        acc_ref[c] += tot

    @pl.when(jc == NJC - 1)
    def _():
        for c in range(2):
            hid = _silu(acc_ref[c] + b1_ref[c])
            comp = jnp.dot(hid.astype(BF16), w2_ref[c], preferred_element_type=F32) + b2_ref[c]
            for h in range(n_kv):
                for par in range(2):
                    out_ref[c, h, par] = comp[h * P:(h + 1) * P,
                                              par * LANES:(par + 1) * LANES].astype(BF16)


def _compress_cache(page_idx, src, cw, l, *, G, P):
    pos_t, w1_t, b1_e, w2_e, b2_e = cw
    n_kv, hh, page = src.shape[2], src.shape[4], src.shape[5]
    JC = 8
    NJC = hh // JC
    grid_spec = pltpu.PrefetchScalarGridSpec(
        num_scalar_prefetch=1,
        grid=(G, NJC),
        in_specs=[
            pl.BlockSpec(memory_space=pl.ANY),
            pl.BlockSpec((None, 2, 2, JC, 1, page), lambda g, j, p: (l, 0, 0, j, 0, 0)),
            pl.BlockSpec((None, 2, JC, 2 * page, 2 * LANES), lambda g, j, p: (l, 0, j, 0, 0)),
            pl.BlockSpec((None, 2, 1, 2 * LANES), lambda g, j, p: (l, 0, 0, 0)),
            pl.BlockSpec((None, 2, 2 * LANES, 2 * LANES), lambda g, j, p: (l, 0, 0, 0)),
            pl.BlockSpec((None, 2, 1, 2 * LANES), lambda g, j, p: (l, 0, 0, 0)),
        ],
        out_specs=pl.BlockSpec((2, n_kv, 2, P, LANES), lambda g, j, p: (0, 0, 0, g, 0)),
        scratch_shapes=[
            pltpu.VMEM((2, 2, n_kv, 2, JC, P, page), F32),
            pltpu.VMEM((2, n_kv * P, 2 * LANES), F32),
            pltpu.SemaphoreType.DMA((2,)),
        ],
    )
    return pl.pallas_call(
        functools.partial(_compress_cache_body, P=P, JC=JC, NJC=NJC, G=G, n_kv=n_kv),
        grid_spec=grid_spec,
        out_shape=jax.ShapeDtypeStruct((2, n_kv, 2, G * P, LANES), BF16),
        compiler_params=_cparams(("arbitrary", "arbitrary")),
        name="compress_cache",
    )(page_idx, src, pos_t, w1_t, b1_e, w2_e, b2_e)


def _compress_cache_weights(cmp_pos, cmp_w1, cmpw):
    depth, _, blk, hd = cmp_pos.shape
    hidden = cmp_w1.shape[-1]
    hh = hd // 2
    pos_t = jnp.tile(cmp_pos.reshape(depth, 2, blk, 2, hh).transpose(0, 1, 3, 4, 2)[:, :, :, :, None, :],
                     (1, 1, 1, 1, 1, 2))
    w1 = cmp_w1.reshape(depth, 2, blk, 2, hh, hidden).transpose(0, 1, 4, 3, 2, 5)
    w1_t = _blockdiag2(w1).reshape(depth, 2, hh, 4 * blk, 2 * hidden)
    _, _, b1_e, w2_e, b2_e = cmpw
    return pos_t, w1_t.astype(BF16), b1_e, w2_e, b2_e


def _stack_heads(cols, lo):
    rows = []
    for cj in cols:
        rows.append(jnp.where(lo, cj, 0.0))
        rows.append(jnp.where(lo, 0.0, cj))
    return jnp.concatenate(rows, axis=0).astype(BF16)


def _block_ids(n, pb):
    return jnp.where(n >= pb, 2 * (n - pb) + 1, 2 * n)


def _nsa_prompt_body(q_ref, z_ref, gt_ref, tc_ref, ta_ref, tb_ref, ks_ref, vs_ref, kw_ref, vw_ref,
                     kc_ref, vc_ref, o_ref, *, tq, tk, T, PB, blk, half, scale):
    qi = pl.program_id(2)
    t0 = qi * tq
    nb = 2 * PB
    shift = blk.bit_length() - 1
    tc, ta, tb = tc_ref[...], ta_ref[...], tb_ref[...]
    lo = lax.broadcasted_iota(jnp.int32, (tq, LANES), 1) < LANES // 2
    q = q_ref[...] * scale
    q0, q1 = q[:, 0:LANES], q[:, LANES:2 * LANES]
    qc = _stack_heads([q0, q1], lo)
    qr = _stack_heads([_rope(q0, tc, ta, tb, half), _rope(q1, tc, ta, tb, half)], lo)
    tpos = t0 + lax.broadcasted_iota(jnp.int32, (tq, 1), 0)
    cur = lax.shift_right_logical(tpos, shift)

    def tile4(x):
        return jnp.concatenate([x, x, x, x], axis=0)

    zrow = jnp.zeros((LANES - nb, LANES), BF16)
    kc = jnp.concatenate([kc_ref[0], kc_ref[1], zrow], axis=0)
    vc = jnp.concatenate([vc_ref[0], vc_ref[1], zrow], axis=0)
    bid = _block_ids(lax.broadcasted_iota(jnp.int32, (nb, tq), 0), PB)
    tpos_l = t0 + lax.broadcasted_iota(jnp.int32, (nb, tq), 1)
    cur_l = lax.shift_right_logical(tpos_l, shift)
    cmask = (bid + 1) * blk - 1 <= tpos_l

    def lanes4(x):
        return jnp.concatenate([x, x, x, x], axis=1)

    cm4 = lanes4(cmask.astype(F32)) > 0.5
    sc = lax.dot_general(kc, qc, NT_DIMS, preferred_element_type=F32)[0:nb]
    sc = jnp.where(cm4, sc, NEG)
    mc = jnp.max(sc, axis=0, keepdims=True)
    ec = jnp.where(cm4, jnp.exp(sc - mc), 0.0)
    pct = ec / jnp.maximum(jnp.sum(ec, axis=0, keepdims=True), 1e-30)
    imp = pct[:, 0:tq] + pct[:, tq:2 * tq] + pct[:, 2 * tq:3 * tq] + pct[:, 3 * tq:4 * tq]

    forced = (bid == 0) | (bid == cur_l) | (bid == cur_l - 1)
    valid = bid <= cur_l
    score = jnp.where(valid, jnp.where(forced, BIG, imp), -BIG)
    cnt = jnp.zeros((nb, tq), F32)
    for i in range(nb):
        other = score[i:i + 1, :]
        bi = 2 * (i % PB) + i // PB
        beats = (other > score) | ((other == score) & (bid > bi))
        cnt = cnt + beats.astype(F32)
    sel_t = ((cnt < TOP_N) & valid).astype(F32)

    zpad = jnp.zeros((LANES - nb, tq), F32)
    pc = jnp.concatenate(
        [jnp.transpose(jnp.concatenate([pct[:, hh * tq:(hh + 1) * tq], zpad], axis=0))
         for hh in range(4)], axis=0)
    oc = jnp.dot(pc.astype(BF16), vc, preferred_element_type=F32)
    sel = jnp.transpose(jnp.concatenate([sel_t, zpad], axis=0)).astype(BF16)

    row_l = lax.broadcasted_iota(jnp.int32, (LANES, tk), 0)
    bid_col = jnp.where(row_l < nb, _block_ids(row_l, PB), -1)
    key_l = lax.broadcasted_iota(jnp.int32, (LANES, tk), 1)
    key_q = lax.broadcasted_iota(jnp.int32, (tq, tk), 1)

    def sel_chunk(ci, carry):
        m, l, acc = carry
        k0 = pl.multiple_of(ci * tk, tk)
        s = lax.dot_general(qr, ks_ref[pl.ds(k0, tk), :], NT_DIMS, preferred_element_type=F32)
        expand = (bid_col == lax.shift_right_logical(key_l + k0, shift)).astype(BF16)
        bm = jnp.dot(sel, expand, preferred_element_type=F32)
        ok = (bm > 0.5) & (key_q + k0 <= tpos)
        s = s + tile4(jnp.where(ok, 0.0, NEG))
        m_new = jnp.maximum(m, jnp.max(s, axis=-1, keepdims=True))
        alpha = jnp.exp(m - m_new)
        p = jnp.exp(s - m_new)
        l = alpha * l + jnp.sum(p, axis=-1, keepdims=True)
        acc = alpha * acc + jnp.dot(p.astype(BF16), vs_ref[pl.ds(k0, tk), :],
                                    preferred_element_type=F32)
        return m_new, l, acc

    nchunks = (t0 + tq + tk - 1) // tk
    init = (jnp.full((4 * tq, 1), NEG, F32), jnp.zeros((4 * tq, 1), F32),
            jnp.zeros((4 * tq, LANES), F32))
    _, ls, accs = lax.fori_loop(0, nchunks, sel_chunk, init)
    osel = accs / ls

    ww = min(WINDOW + tq, T)
    ws = pl.multiple_of(jnp.clip(t0 - WINDOW, 0, T - ww), tq)
    sw = lax.dot_general(qr, kw_ref[pl.ds(ws, ww), :], NT_DIMS, preferred_element_type=F32)
    dl = tpos - (ws + lax.broadcasted_iota(jnp.int32, (tq, ww), 1))
    sw = sw + tile4(jnp.where((dl >= 0) & (dl < WINDOW), 0.0, NEG))
    pw = jnp.exp(sw - jnp.max(sw, axis=-1, keepdims=True))
    ow = jnp.dot(pw.astype(BF16), vw_ref[pl.ds(ws, ww), :], preferred_element_type=F32)
    ow = ow / jnp.sum(pw, axis=-1, keepdims=True)

    g = gt_ref[...]

    def gcol(br):
        return jnp.concatenate([g[:, 3 * hh + br:3 * hh + br + 1] for hh in range(4)], axis=0)

    o = gcol(0) * oc + gcol(1) * osel + gcol(2) * ow
    p0 = jnp.where(lo, o[0:tq], o[tq:2 * tq])
    p1 = jnp.where(lo, o[2 * tq:3 * tq], o[3 * tq:4 * tq])
    o_ref[...] = (jnp.concatenate([p0, p1], axis=1) * _silu(z_ref[...])).astype(o_ref.dtype)


def _nsa_prompt(proj, gates, tabs, ks, vs, kw, vw, cmp_e, *, B, T, n_kv, head_dim, q_col, z_col,
                blk, tq, tk):
    nq = T // tq
    gw = 2 * LANES
    PB = T // (2 * blk)
    row = lambda b, h, i: b * nq + i
    tab = pl.BlockSpec((tq, LANES), lambda b, h, i: (i, 0))
    kvs = pl.BlockSpec((None, None, T, LANES), lambda b, h, i: (b, h, 0, 0))
    return pl.pallas_call(
        functools.partial(_nsa_prompt_body, tq=tq, tk=tk, T=T, PB=PB, blk=blk,
                          half=head_dim // 8, scale=head_dim ** -0.5),
        grid=(B, n_kv, nq),
        in_specs=[
            pl.BlockSpec((tq, gw), lambda b, h, i: (row(b, h, i), q_col // gw + h)),
            pl.BlockSpec((tq, gw), lambda b, h, i: (row(b, h, i), z_col // gw + h)),
            pl.BlockSpec((None, None, tq, LANES), lambda b, h, i: (b, h, i, 0)),
            tab, tab, tab,
            kvs, kvs, kvs, kvs,
            pl.BlockSpec((None, None, 2, PB, LANES), lambda b, h, i: (0, h, 0, b, 0)),
            pl.BlockSpec((None, None, 2, PB, LANES), lambda b, h, i: (1, h, 0, b, 0)),
        ],
        out_specs=pl.BlockSpec((tq, gw), lambda b, h, i: (row(b, h, i), h)),
        out_shape=jax.ShapeDtypeStruct((B * T, n_kv * gw), BF16),
        compiler_params=_cparams(("parallel", "parallel", "arbitrary")),
        name="nsa_prompt",
    )(proj, proj, gates, *tabs, ks, vs, kw, vw, cmp_e, cmp_e)


def _outproj_body(x_ref, a_ref, b_ref, wa_ref, wb_ref, g_ref, o_ref):
    acc = jnp.dot(a_ref[...].astype(BF16), wa_ref[...], preferred_element_type=F32)
    acc = acc + jnp.dot(b_ref[...].astype(BF16), wb_ref[...], preferred_element_type=F32)
    ms = jnp.mean(acc * acc, axis=-1, keepdims=True)
    o_ref[...] = x_ref[...] + (acc * lax.rsqrt(ms + EPS)) * g_ref[...]


def _outproj(x, a, b, w_out, g, l, *, tm):
    M, D = x.shape
    ka, kb = a.shape[1], b.shape[1]
    return pl.pallas_call(
        _outproj_body,
        grid=(M // tm,),
        in_specs=[
            pl.BlockSpec((tm, D), lambda i: (i, 0)),
            pl.BlockSpec((tm, ka), lambda i: (i, 0)),
            pl.BlockSpec((tm, kb), lambda i: (i, 0)),
            pl.BlockSpec((None, ka, D), lambda i: (l, 0, 0)),
            pl.BlockSpec((None, kb, D), lambda i: (l, ka // kb, 0)),
            pl.BlockSpec((None, 1, D), lambda i: (l, 0, 0)),
        ],
        out_specs=pl.BlockSpec((tm, D), lambda i: (i, 0)),
        out_shape=jax.ShapeDtypeStruct((M, D), F32),
        compiler_params=_cparams(("parallel",)),
        name="outproj",
    )(x, a, b, w_out, w_out, g)


def _sample_prep_body(p_ref, st_ref, w_ref, cb_ref, lg_ref, lb_ref, gb_ref,
                      tc_ref, ta_ref, tb_ref,
                      y_ref, cn_ref, kvn_ref, win_ref, qc_ref, qr_ref, gt_ref, zs_ref,
                      *, C, HD, kvw, half, scale):
    nb = p_ref.shape[0]
    taps = w_ref.shape[0]
    tc, ta, tb = tc_ref[...], ta_ref[...], tb_ref[...]
    glu = p_ref[:, 0:C] * jax.nn.sigmoid(p_ref[:, C:2 * C])
    c = w_ref[taps - 1:taps, :] * glu + cb_ref[...]
    for k in range(taps - 1):
        c = c + w_ref[k:k + 1, :] * st_ref[k]
    cn_ref[0:taps - 2] = st_ref[1:taps - 1]
    cn_ref[taps - 2] = glu
    mu = jnp.mean(c, axis=-1, keepdims=True)
    d = c - mu
    var = jnp.mean(d * d, axis=-1, keepdims=True)
    yl = (d * lax.rsqrt(var + EPS)) * lg_ref[...] + lb_ref[...]
    y_ref[...] = _silu(yl) * _silu(p_ref[:, 2 * C:3 * C])

    q_col = 3 * C
    z_col = q_col + HD
    kv_col = z_col + HD
    for j in range(HD // LANES):
        qj = p_ref[:, q_col + j * LANES:q_col + (j + 1) * LANES] * scale
        qc_ref[:, j * LANES:(j + 1) * LANES] = qj
        qr_ref[:, j * LANES:(j + 1) * LANES] = _rope(qj, tc, ta, tb, half)
    zs_ref[...] = _silu(p_ref[:, z_col:z_col + HD])
    kvn_ref[:, 0:2 * kvw] = p_ref[:, kv_col:kv_col + 2 * kvw]
    kvn_ref[:, 3 * kvw:4 * kvw] = p_ref[:, kv_col + 3 * kvw:kv_col + 4 * kvw]
    win_ref[:, kvw:2 * kvw] = p_ref[:, kv_col + 5 * kvw:kv_col + 6 * kvw]
    for j in range(kvw // LANES):
        s0 = kv_col + 2 * kvw + j * LANES
        kvn_ref[:, 2 * kvw + j * LANES:2 * kvw + (j + 1) * LANES] = _rope(
            p_ref[:, s0:s0 + LANES], tc, ta, tb, half)
        w0 = kv_col + 4 * kvw + j * LANES
        win_ref[:, j * LANES:(j + 1) * LANES] = _rope(p_ref[:, w0:w0 + LANES], tc, ta, tb, half)
    gl_col = kv_col + 6 * kvw
    gt_ref[...] = jax.nn.sigmoid(p_ref[:, gl_col:gl_col + LANES] + gb_ref[...])


def _sample_prep(proj, state_conv, conv_w, conv_b, ln_g, ln_b, gate_b, tabs, l,
                 *, C, HD, kvw, head_dim):
    nb = proj.shape[0]
    taps = conv_w.shape[1]
    whole = lambda shape: pl.BlockSpec(shape, lambda i: (0,) * len(shape))
    vecc = pl.BlockSpec((None, 1, C), lambda i: (l, 0, 0))
    tab = whole((1, LANES))
    shapes = [
        ((nb, C), F32), ((taps - 1, nb, C), F32), ((nb, 4 * kvw), F32), ((nb, 2 * kvw), F32),
        ((nb, HD), F32), ((nb, HD), F32), ((nb, LANES), F32), ((nb, HD), F32),
    ]
    return pl.pallas_call(
        functools.partial(_sample_prep_body, C=C, HD=HD, kvw=kvw, half=head_dim // 8,
                          scale=head_dim ** -0.5),
        grid=(1,),
        in_specs=[
            whole(proj.shape),
            pl.BlockSpec((None, taps - 1, nb, C), lambda i: (l, 0, 0, 0)),
            pl.BlockSpec((None, taps, C), lambda i: (l, 0, 0)),
            vecc, vecc, vecc,
            pl.BlockSpec((None, 1, LANES), lambda i: (l, 0, 0)),
            tab, tab, tab,
        ],
        out_specs=[whole(s) for s, _ in shapes],
        out_shape=[jax.ShapeDtypeStruct(s, d) for s, d in shapes],
        compiler_params=_cparams(("arbitrary",)),
        name="sample_prep",
    )(proj, state_conv, conv_w, conv_b, ln_g, ln_b, gate_b, *tabs)


def _sample_cmp_body(qc_ref, kc_ref, vc_ref, oc_ref, idx_ref, *, P, n_kv):
    b = pl.program_id(0)
    nb = 2 * P
    qrow = qc_ref[pl.ds(b, 1), :]
    lo = lax.broadcasted_iota(jnp.int32, (1, LANES), 1) < LANES // 2
    zero4 = jnp.zeros((4, LANES), F32)
    imps = []
    for h in range(n_kv):
        c0 = qrow[:, 2 * h * LANES:(2 * h + 1) * LANES]
        c1 = qrow[:, (2 * h + 1) * LANES:(2 * h + 2) * LANES]
        qs = jnp.concatenate([jnp.where(lo, c0, 0.0), jnp.where(lo, 0.0, c0),
                              jnp.where(lo, c1, 0.0), jnp.where(lo, 0.0, c1), zero4],
                             axis=0).astype(BF16)
        kc = jnp.concatenate([kc_ref[h, 0], kc_ref[h, 1]], axis=0)
        vc = jnp.concatenate([vc_ref[h, 0], vc_ref[h, 1]], axis=0)
        s = lax.dot_general(qs, kc, NT_DIMS, preferred_element_type=F32)
        e = jnp.exp(s - jnp.max(s, axis=-1, keepdims=True))
        pc = e / jnp.maximum(jnp.sum(e, axis=-1, keepdims=True), 1e-30)
        oc_ref[h] = jnp.dot(pc.astype(BF16), vc, preferred_element_type=F32)
        imps.append(pc[0:1] + pc[1:2] + pc[2:3] + pc[3:4])
    imp = jnp.concatenate(imps, axis=0)
    bid = _block_ids(lax.broadcasted_iota(jnp.int32, (n_kv, nb), 1), P)
    forced = (bid == 0) | (bid == nb - 1)
    score = jnp.where(forced, BIG, imp)
    cnt = (score < BIG).astype(F32)
    for i in range(nb):
        col = score[:, i:i + 1]
        beats = (col > score) | ((col == score) & (bid > 2 * (i % P) + i // P))
        cnt = cnt + beats.astype(F32)
    lane = lax.broadcasted_iota(jnp.int32, (n_kv, LANES), 1)
    res = jnp.full((n_kv, LANES), -1.0, F32)
    bid1 = (bid + 1).astype(F32)
    for k in range(TOP_N):
        v = jnp.sum(jnp.where(cnt == k, bid1, 0.0), axis=-1, keepdims=True) - 1.0
        res = jnp.where(lane == k, v, res)
    idx_ref[...] = res.astype(jnp.int32)


def _sample_cmp(qc, cmp_e, *, DB, P, n_kv):
    ce = lambda c: pl.BlockSpec((None, n_kv, 2, P, LANES), lambda b: (c, 0, 0, b, 0))
    return pl.pallas_call(
        functools.partial(_sample_cmp_body, P=P, n_kv=n_kv),
        grid=(DB,),
        in_specs=[pl.BlockSpec(qc.shape, lambda b: (0, 0)), ce(0), ce(1)],
        out_specs=[pl.BlockSpec((None, n_kv, 8, LANES), lambda b: (b, 0, 0, 0)),
                   pl.BlockSpec((None, n_kv, LANES), lambda b: (b, 0, 0))],
        out_shape=[jax.ShapeDtypeStruct((DB, n_kv, 8, LANES), F32),
                   jax.ShapeDtypeStruct((DB, n_kv, LANES), jnp.int32)],
        compiler_params=_cparams(("parallel",)),
        name="sample_cmp",
    )(qc, cmp_e, cmp_e)


def _sample_attn_body(idx_ref, pages_ref, qr_ref, gt_ref, zs_ref, oc_ref, kvn_ref, wr_ref, cw_ref,
                      cache_ref, o_ref, wn_ref, selbuf, sem, *, P, n_kv, hd, kvw, nwin):
    b = pl.program_id(0)

    def page_copy(pg, h, slot):
        return pltpu.make_async_copy(cache_ref.at[pg, pl.ds(2, 2), h], selbuf.at[slot], sem.at[0])

    for h in range(n_kv):
        for k in range(TOP_N):
            ix = jnp.maximum(idx_ref[(b * n_kv + h) * LANES + k], 0)
            page_copy(pages_ref[b * P + lax.shift_right_logical(ix, 1)], h, h * TOP_N + k).start()

    newrow = wr_ref[pl.ds(b, 1), :]
    eye = (lax.broadcasted_iota(jnp.int32, (hd, hd), 0)
           == lax.broadcasted_iota(jnp.int32, (hd, hd), 1))
    last = lax.broadcasted_iota(jnp.int32, (hd, nwin), 1) == nwin - 1
    for c in range(2):
        for h in range(n_kv):
            v = newrow[:, c * kvw + h * hd:c * kvw + (h + 1) * hd]
            col = jnp.sum(jnp.where(eye, v, 0.0), axis=1, keepdims=True)
            wn_ref[c, h] = jnp.where(last, col, pltpu.roll(cw_ref[c, h], nwin - 1, 1))

    for s in range(n_kv * TOP_N):
        page_copy(0, 0, s).wait()

    qrow = qr_ref[pl.ds(b, 1), :]
    kvrow = kvn_ref[pl.ds(b, 1), :]
    g = gt_ref[pl.ds(b, 1), :]
    zero4 = jnp.zeros((4, hd), F32)
    zero41 = jnp.zeros((4, 1), F32)
    page = selbuf.shape[3]
    rows = page // 2
    nk = TOP_N * page
    key = lax.broadcasted_iota(jnp.int32, (1, nk), 1)
    slot_of_key = key // page
    half_of_key = (key // rows) & 1
    pieces = []
    for h in range(n_kv):
        qh = jnp.concatenate([qrow[:, (4 * h + gg) * hd:(4 * h + gg + 1) * hd] for gg in range(4)]
                             + [zero4], axis=0)
        qb = qh.astype(BF16)
        ks_new = kvrow[:, 2 * kvw + h * hd:2 * kvw + (h + 1) * hd]
        vs_new = kvrow[:, 3 * kvw + h * hd:3 * kvw + (h + 1) * hd]
        kt = jnp.concatenate([selbuf[h * TOP_N + k, 0] for k in range(TOP_N)], axis=1).astype(BF16)
        vt = jnp.concatenate([selbuf[h * TOP_N + k, 1] for k in range(TOP_N)], axis=1).astype(BF16)
        want = jnp.full((1, nk), -1, jnp.int32)
        for k in range(TOP_N):
            ix = idx_ref[(b * n_kv + h) * LANES + k]
            want = jnp.where(slot_of_key == k, jnp.where(ix >= 0, ix & 1, -1), want)
        s_all = jnp.dot(qb, kt, preferred_element_type=F32)
        s_all = jnp.where(half_of_key == want, s_all, NEG)
        s_new = jnp.sum(qb.astype(F32) * ks_new.astype(BF16).astype(F32), axis=-1, keepdims=True)
        m = jnp.maximum(jnp.max(s_all, axis=-1, keepdims=True), s_new)
        e = jnp.exp(s_all - m)
        e_new = jnp.exp(s_new - m)
        lsum = jnp.sum(e, axis=-1, keepdims=True) + e_new
        acc = e_new.astype(BF16).astype(F32) * vs_new.astype(BF16).astype(F32)
        acc = acc + lax.dot_general(e.astype(BF16), vt, NT_DIMS, preferred_element_type=F32)
        osel = acc / lsum

        sw = jnp.dot(qb, wn_ref[0, h].astype(BF16), preferred_element_type=F32)
        pw = jnp.exp(sw - jnp.max(sw, axis=-1, keepdims=True))
        ow = lax.dot_general(pw.astype(BF16), wn_ref[1, h].astype(BF16), NT_DIMS,
                             preferred_element_type=F32)
        ow = ow / jnp.sum(pw, axis=-1, keepdims=True)

        def gcol(br):
            return jnp.concatenate(
                [g[:, (4 * h + gg) * 3 + br:(4 * h + gg) * 3 + br + 1] for gg in range(4)]
                + [zero41], axis=0)

        o = gcol(0) * oc_ref[h][:, 0:hd] + gcol(1) * osel + gcol(2) * ow
        for gg in range(4):
            pieces.append(o[gg:gg + 1, :])
    o_ref[pl.ds(b, 1), :] = jnp.concatenate(pieces, axis=1) * zs_ref[pl.ds(b, 1), :]


def _sample_attn(idx, pages, qr, gates, zs, oc, kvn, winrow, cache_win_t, cache_t, l,
                 *, DB, P, n_kv, head_dim):
    kvw = n_kv * head_dim
    nwin = cache_win_t.shape[-1]
    page = cache_t.shape[-1]
    win_blk = (None, 2, n_kv, head_dim, nwin)
    whole = lambda a: pl.BlockSpec(a.shape, lambda b, i, p: (0,) * a.ndim)
    grid_spec = pltpu.PrefetchScalarGridSpec(
        num_scalar_prefetch=2,
        grid=(DB,),
        in_specs=[
            whole(qr), whole(gates), whole(zs),
            pl.BlockSpec((None, n_kv, 8, LANES), lambda b, i, p: (b, 0, 0, 0)),
            whole(kvn), whole(winrow),
            pl.BlockSpec(win_blk, lambda b, i, p: (l * DB + b, 0, 0, 0, 0)),
            pl.BlockSpec(memory_space=pl.ANY),
        ],
        out_specs=[
            pl.BlockSpec(qr.shape, lambda b, i, p: (0, 0)),
            pl.BlockSpec(win_blk, lambda b, i, p: (b, 0, 0, 0, 0)),
        ],
        scratch_shapes=[
            pltpu.VMEM((n_kv * TOP_N, 2, head_dim, page), F32),
            pltpu.SemaphoreType.DMA((1,)),
        ],
    )
    return pl.pallas_call(
        functools.partial(_sample_attn_body, P=P, n_kv=n_kv, hd=head_dim, kvw=kvw, nwin=nwin),
        grid_spec=grid_spec,
        out_shape=[jax.ShapeDtypeStruct(qr.shape, F32),
                   jax.ShapeDtypeStruct((DB, 2, n_kv, head_dim, nwin), F32)],
        compiler_params=_cparams(("arbitrary",)),
        name="sample_attn",
    )(idx, pages, qr, gates, zs, oc, kvn, winrow, cache_win_t, cache_t)


def kernel(x_prompt, x_sample, cache_kv_pages, cache_win, state_conv, page_table, w_in, w_out,
           norm_pre, norm_post, conv_w, conv_b, conv_ln_g, conv_ln_b, cmp_pos, cmp_w1, cmp_b1,
           cmp_w2, cmp_b2, gate_b):
    B, S, D = x_prompt.shape
    DB, dec_seq, _ = x_sample.shape
    depth, n_pool, page, _, n_kv, head_dim = cache_kv_pages.shape
    n_pages = page_table.shape[1]
    past_len = n_pages * page
    C = conv_w.shape[-1]
    taps = conv_w.shape[1]
    n_heads = gate_b.shape[-1] // 3
    HD = n_heads * head_dim
    kvw = n_kv * head_dim
    blk = cmp_pos.shape[2]
    n_main = 3 * C + 2 * HD + 6 * kvw
    nwin = cache_win.shape[2]
    gpg = 3 * (n_heads // n_kv)
    assert dec_seq == 1 and page == 2 * blk and past_len % blk == 0
    assert w_in.shape[-1] == n_main + 3 * n_heads and n_heads == 4 * n_kv and head_dim * 2 == LANES
    assert nwin == WINDOW and S >= WINDOW and past_len // blk >= TOP_N and S % page == 0

    w_out_b = w_out.astype(BF16)
    gate_bp = jnp.pad(gate_b, ((0, 0), (0, LANES - 3 * n_heads))).reshape(depth, 1, LANES)
    cmpw = _compress_weights(cmp_pos, cmp_w1, cmp_b1, cmp_w2, cmp_b2)
    cmpw_c = _compress_cache_weights(cmp_pos, cmp_w1, cmpw)
    vec3 = lambda v: v.reshape(depth, 1, v.shape[-1])
    norm_pre3, norm_post3 = vec3(norm_pre), vec3(norm_post)
    conv_b3, ln_g3, ln_b3 = vec3(conv_b), vec3(conv_ln_g), vec3(conv_ln_b)

    tabs_p = _rope_tables(jnp.arange(S, dtype=jnp.int32), head_dim)
    tabs_s = _rope_tables(jnp.full((1,), past_len, jnp.int32), head_dim)

    cache_t = cache_kv_pages.transpose(0, 1, 3, 4, 5, 2).reshape(depth * n_pool, 4, n_kv, head_dim, page)
    cache_t6 = cache_t.reshape(depth * n_pool, 4, n_kv, 2, head_dim // 2, page)
    cache_win_t = cache_win.transpose(0, 1, 3, 4, 5, 2).reshape(depth * DB, 2, n_kv, head_dim, nwin)
    state_t = state_conv.transpose(0, 2, 1, 3)
    w_in_t = w_in.transpose(0, 2, 1).astype(BF16)
    pages_prompt = jnp.arange(B * S // page, dtype=jnp.int32)
    conv_zero = jnp.zeros((B, taps - 1, C), F32)

    xp = x_prompt.reshape(B * S, D)
    xs = x_sample.reshape(DB, D)
    tm_in = 1024 if (B * S) % 1024 == 0 else 256
    tm_out = 512 if (B * S) % 512 == 0 else 256
    outs = [[] for _ in range(6)]
    for l in range(depth):
        proj = _inproj(xp, norm_pre3, w_in_t, l, tm=tm_in, tn=512)
        yconv, conv_new = _conv_prompt(proj, conv_zero, conv_w, conv_b3, ln_g3, ln_b3, l,
                                       B=B, T=S, C=C, tt=256)
        kvn, winr, ks, vs, kw, vw, gates = _assemble_prompt(
            proj, gate_bp, tabs_p, l, B=B, T=S, n_kv=n_kv, head_dim=head_dim,
            kv_col=3 * C + 2 * HD, gl_col=n_main, tt=256, gpg=gpg, nwin=nwin)
        cmp_e = _compress(pages_prompt, proj.reshape(B * S // page, 2, 2, blk // 2, proj.shape[1]),
                          cmpw, l, G=1, P=B * S // page, col0=3 * C + 2 * HD, n_kv=n_kv,
                          head_dim=head_dim)
        attn = _nsa_prompt(proj, gates, tabs_p, ks, vs, kw, vw, cmp_e, B=B, T=S, n_kv=n_kv,
                           head_dim=head_dim, q_col=3 * C, z_col=3 * C + HD, blk=blk, tq=128, tk=512)
        xp = _outproj(xp, yconv, attn, w_out_b, norm_post3, l, tm=tm_out)
        outs[0].append(kvn.transpose(0, 4, 1, 2, 3))
        outs[2].append(winr.transpose(0, 4, 1, 2, 3))
        outs[4].append(conv_new)

        proj_s = _inproj(xs, norm_pre3, w_in_t, l, tm=DB, tn=512)
        (yconv_s, conv_new_s, kvn_s, winrow_s, qc_s, qr_s, gates_s, zs_s) = _sample_prep(
            proj_s, state_t, conv_w, conv_b3, ln_g3, ln_b3, gate_bp, tabs_s, l,
            C=C, HD=HD, kvw=kvw, head_dim=head_dim)
        pages_s = (page_table + l * n_pool).reshape(-1).astype(jnp.int32)
        cmp_s = _compress_cache(pages_s, cache_t6, cmpw_c, l, G=DB, P=n_pages)
        oc_s, idx_s = _sample_cmp(qc_s, cmp_s, DB=DB, P=n_pages, n_kv=n_kv)
        attn_s, win_new_s = _sample_attn(idx_s.reshape(-1), pages_s, qr_s, gates_s, zs_s, oc_s, kvn_s,
                                         winrow_s, cache_win_t, cache_t, l,
                                         DB=DB, P=n_pages, n_kv=n_kv, head_dim=head_dim)
        xs = _outproj(xs, yconv_s, attn_s, w_out_b, norm_post3, l, tm=DB)
        outs[1].append(kvn_s.reshape(DB, 1, 4, n_kv, head_dim))
        outs[3].append(win_new_s.transpose(0, 4, 1, 2, 3))
        outs[5].append(conv_new_s.transpose(1, 0, 2))

    return (xp.reshape(B, S, D), xs.reshape(DB, 1, D), jnp.stack(outs[0]), jnp.stack(outs[1]),
            jnp.stack(outs[2]), jnp.stack(outs[3]), jnp.stack(outs[4]), jnp.stack(outs[5]))
```

```python
import functools

import jax
import jax.numpy as jnp
from jax import lax
from jax.experimental import pallas as pl
from jax.experimental.pallas import tpu as pltpu

F32 = jnp.float32
BF16 = jnp.bfloat16

TOP_N = 16
WINDOW = 512
ROPE_THETA = 500000.0
EPS = 1e-6
BIG = 1e9
NEG = -1e30
LANES = 128
VMEM_LIMIT = 56 * 1024 * 1024

NT_DIMS = (((1,), (1,)), ((), ()))


def _silu(x):
    return x * jax.nn.sigmoid(x)


def _cparams(sem):
    return pltpu.CompilerParams(dimension_semantics=sem, vmem_limit_bytes=VMEM_LIMIT)


def _inproj_body(x_ref, g_ref, w_ref, o_ref, u_ref, *, n_in, tn):
    j = pl.program_id(1)

    @pl.when(j == 0)
    def _():
        x = x_ref[...]
        ms = jnp.mean(x * x, axis=-1, keepdims=True)
        u_ref[...] = ((x * lax.rsqrt(ms + EPS)) * g_ref[...]).astype(u_ref.dtype)

    nj = pl.num_programs(1)

    @pl.when(j < nj - 1)
    def _():
        o_ref[...] = lax.dot_general(u_ref[...].astype(BF16), w_ref[...].astype(BF16), NT_DIMS,
                                     preferred_element_type=F32)

    @pl.when(j == nj - 1)
    def _():
        row = j * tn + lax.broadcasted_iota(jnp.int32, (tn, 1), 0)
        w = jnp.where(row < n_in, w_ref[...], 0.0).astype(BF16)
        o_ref[...] = lax.dot_general(u_ref[...].astype(BF16), w, NT_DIMS, preferred_element_type=F32)


def _inproj(x, g, w_in_t, l, *, tm, tn):
    M, D = x.shape
    n_in = w_in_t.shape[1]
    nj = pl.cdiv(n_in, tn)
    u_dtype = BF16 if tm % 16 == 0 else F32
    return pl.pallas_call(
        functools.partial(_inproj_body, n_in=n_in, tn=tn),
        grid=(M // tm, nj),
        in_specs=[
            pl.BlockSpec((tm, D), lambda i, j: (i, 0)),
            pl.BlockSpec((None, 1, D), lambda i, j: (l, 0, 0)),
            pl.BlockSpec((None, tn, D), lambda i, j: (l, j, 0)),
        ],
        out_specs=pl.BlockSpec((tm, tn), lambda i, j: (i, j)),
        out_shape=jax.ShapeDtypeStruct((M, nj * tn), F32),
        scratch_shapes=[pltpu.VMEM((tm, D), u_dtype)],
        compiler_params=_cparams(("parallel", "arbitrary")),
        name="inproj",
    )(x, g, w_in_t)


def _conv_body(av_ref, ag_ref, z_ref, buf_ref, w_ref, cb_ref, lg_ref, lb_ref,
               y_ref, cn_ref, ap_ref, zs_ref, cv_ref, *, tt, nt, taps, rc, cc):
    t = pl.program_id(1)
    hist = -(-(taps - 1) // 8) * 8
    off = hist - (taps - 1)
    C = ap_ref.shape[1]

    @pl.when(t == 0)
    def _():
        ap_ref[0:off, :] = jnp.zeros((off, C), F32)
        ap_ref[off:hist, :] = buf_ref[...]

    @pl.when(t > 0)
    def _():
        ap_ref[0:hist, :] = ap_ref[tt:tt + hist, :]

    ap_ref[hist:hist + tt, :] = av_ref[...] * jax.nn.sigmoid(ag_ref[...])

    nz = zs_ref.shape[1]
    for s in range(1, 8):
        zs_ref[s - 1] = ap_ref[s:s + nz, :]
    for r0 in range(0, tt, rc):
        for c0 in range(0, C, cc):
            acc = jnp.zeros((rc, cc), F32)
            for k in range(taps):
                s, a = (k + off) % 8, (k + off) // 8
                src = ap_ref if s == 0 else zs_ref.at[s - 1]
                acc = acc + w_ref[k:k + 1, c0:c0 + cc] * src[r0 + 8 * a:r0 + 8 * a + rc, c0:c0 + cc]
            cv_ref[r0:r0 + rc, c0:c0 + cc] = acc
        c = cv_ref[r0:r0 + rc, :] + cb_ref[...]
        mu = jnp.mean(c, axis=-1, keepdims=True)
        d = c - mu
        var = jnp.mean(d * d, axis=-1, keepdims=True)
        yl = (d * lax.rsqrt(var + EPS)) * lg_ref[...] + lb_ref[...]
        y_ref[r0:r0 + rc, :] = (_silu(yl) * _silu(z_ref[r0:r0 + rc, :])).astype(y_ref.dtype)

    @pl.when(t == nt - 1)
    def _():
        cn_ref[...] = ap_ref[tt + off:tt + hist, :]


def _conv_prompt(proj, conv_buf, conv_w, conv_b, ln_g, ln_b, l, *, B, T, C, tt):
    taps = conv_w.shape[1]
    nt = T // tt
    hist = -(-(taps - 1) // 8) * 8
    vec = pl.BlockSpec((None, 1, C), lambda b, t: (l, 0, 0))
    return pl.pallas_call(
        functools.partial(_conv_body, tt=tt, nt=nt, taps=taps, rc=32, cc=min(C, 4 * LANES)),
        grid=(B, nt),
        in_specs=[
            pl.BlockSpec((tt, C), lambda b, t: (b * nt + t, 0)),
            pl.BlockSpec((tt, C), lambda b, t: (b * nt + t, 1)),
            pl.BlockSpec((tt, C), lambda b, t: (b * nt + t, 2)),
            pl.BlockSpec((None, taps - 1, C), lambda b, t: (b, 0, 0)),
            pl.BlockSpec((None, taps, C), lambda b, t: (l, 0, 0)),
            vec, vec, vec,
        ],
        out_specs=[
            pl.BlockSpec((tt, C), lambda b, t: (b * nt + t, 0)),
            pl.BlockSpec((None, taps - 1, C), lambda b, t: (b, 0, 0)),
        ],
        out_shape=[jax.ShapeDtypeStruct((B * T, C), BF16),
                   jax.ShapeDtypeStruct((B, taps - 1, C), F32)],
        scratch_shapes=[pltpu.VMEM((tt + hist, C), F32), pltpu.VMEM((7, tt + hist - 8, C), F32),
                        pltpu.VMEM((tt, C), F32)],
        compiler_params=_cparams(("parallel", "arbitrary")),
        name="conv_prompt",
    )(proj, proj, proj, conv_buf, conv_w, conv_b, ln_g, ln_b)


def _rope_tables(pos, head_dim):
    rope_dim = head_dim // 4
    half = rope_dim // 2
    inv = ROPE_THETA ** (-jnp.arange(half, dtype=F32) / half)
    ang = pos.astype(F32)[:, None] * inv[None, :]
    cos, sin = jnp.cos(ang), jnp.sin(ang)
    d = jnp.arange(LANES) % head_dim
    f = d % half
    cosl, sinl = cos[:, f], sin[:, f]
    c = jnp.where(d < rope_dim, cosl, 1.0)
    a = jnp.where(d < half, -sinl, 0.0)
    b = jnp.where((d >= half) & (d < rope_dim), sinl, 0.0)
    return c.astype(F32), a.astype(F32), b.astype(F32)


def _rope(x, c, a, b, half):
    return x * c + pltpu.roll(x, LANES - half, 1) * a + pltpu.roll(x, half, 1) * b


def _dup_head(x, odd, lo):
    r = pltpu.roll(x, LANES // 2, 1)
    return jnp.where(lo, r, x) if odd else jnp.where(lo, x, r)


def _assemble_body(a_ref, b_ref, c_ref, gl_ref, gb_ref, tc_ref, ta_ref, tb_ref,
                   kvn_ref, win_ref, ks_ref, vs_ref, kw_ref, vw_ref, gt_ref, *, n_kv, half, gpg):
    tc, ta, tb = tc_ref[...], ta_ref[...], tb_ref[...]
    tt = tc.shape[0]
    lo = lax.broadcasted_iota(jnp.int32, (tt, LANES), 1) < LANES // 2
    kvw = n_kv * LANES // 2
    ncol = kvw // LANES

    hd = LANES // 2

    def put_t(ref, kind, j, x):
        xt = jnp.transpose(x)
        ref[kind, 2 * j] = xt[0:hd]
        ref[kind, 2 * j + 1] = xt[hd:2 * hd]

    a = a_ref[...]
    b = b_ref[...]
    c = c_ref[...]
    ksr, kwr = [], []
    for j in range(ncol):
        ksr.append(_rope(b[:, j * LANES:(j + 1) * LANES], tc, ta, tb, half))
        kwr.append(_rope(c[:, j * LANES:(j + 1) * LANES], tc, ta, tb, half))
        put_t(kvn_ref, 0, j, a[:, j * LANES:(j + 1) * LANES])
        put_t(kvn_ref, 1, j, a[:, kvw + j * LANES:kvw + (j + 1) * LANES])
        put_t(kvn_ref, 2, j, ksr[j])
        put_t(kvn_ref, 3, j, b[:, kvw + j * LANES:kvw + (j + 1) * LANES])
        put_t(win_ref, 0, j, kwr[j])
        put_t(win_ref, 1, j, c[:, kvw + j * LANES:kvw + (j + 1) * LANES])

    g = jax.nn.sigmoid(gl_ref[...] + gb_ref[...])
    for h in range(n_kv):
        j, odd = h // 2, h % 2
        ks_ref[h] = _dup_head(ksr[j], odd, lo).astype(BF16)
        vs_ref[h] = _dup_head(b[:, kvw + j * LANES:kvw + (j + 1) * LANES], odd, lo).astype(BF16)
        kw_ref[h] = _dup_head(kwr[j], odd, lo).astype(BF16)
        vw_ref[h] = _dup_head(c[:, kvw + j * LANES:kvw + (j + 1) * LANES], odd, lo).astype(BF16)
        gt_ref[h] = g if h == 0 else pltpu.roll(g, LANES - gpg * h, 1)


def _assemble_prompt(proj, gate_b, tabs, l, *, B, T, n_kv, head_dim, kv_col, gl_col, tt, gpg, nwin):
    kvw = n_kv * head_dim
    assert nwin % tt == 0 and T >= nwin
    nt = T // tt
    cb = kv_col // (2 * kvw)
    row = lambda b, t: b * nt + t
    tab = pl.BlockSpec((tt, LANES), lambda b, t: (t, 0))
    exp = pl.BlockSpec((None, n_kv, tt, LANES), lambda b, t: (b, 0, t, 0))
    exp_shape = jax.ShapeDtypeStruct((B, n_kv, T, LANES), BF16)
    return pl.pallas_call(
        functools.partial(_assemble_body, n_kv=n_kv, half=head_dim // 8, gpg=gpg),
        grid=(B, nt),
        in_specs=[
            pl.BlockSpec((tt, 2 * kvw), lambda b, t: (row(b, t), cb)),
            pl.BlockSpec((tt, 2 * kvw), lambda b, t: (row(b, t), cb + 1)),
            pl.BlockSpec((tt, 2 * kvw), lambda b, t: (row(b, t), cb + 2)),
            pl.BlockSpec((tt, LANES), lambda b, t: (row(b, t), gl_col // LANES)),
            pl.BlockSpec((None, 1, LANES), lambda b, t: (l, 0, 0)),
            tab, tab, tab,
        ],
        out_specs=[
            pl.BlockSpec((None, 4, n_kv, head_dim, tt), lambda b, t: (b, 0, 0, 0, t)),
            pl.BlockSpec((None, 2, n_kv, head_dim, tt),
                         lambda b, t: (b, 0, 0, 0, jnp.maximum(t - (nt - nwin // tt), 0))),
            exp, exp, exp, exp,
            pl.BlockSpec((None, n_kv, tt, LANES), lambda b, t: (b, 0, t, 0)),
        ],
        out_shape=[
            jax.ShapeDtypeStruct((B, 4, n_kv, head_dim, T), F32),
            jax.ShapeDtypeStruct((B, 2, n_kv, head_dim, nwin), F32),
            exp_shape, exp_shape, exp_shape, exp_shape,
            jax.ShapeDtypeStruct((B, n_kv, T, LANES), F32),
        ],
        compiler_params=_cparams(("parallel", "arbitrary")),
        name="assemble_prompt",
    )(proj, proj, proj, proj, gate_b, *tabs)


def _compress_body(pidx_ref, src_ref, pos_ref, w1_ref, b1_ref, w2_ref, b2_ref, out_ref,
                   xbuf, acc_ref, sem, *, P, JC, NJC, G, col0, cw):
    g = pl.program_id(0)
    jc = pl.program_id(1)
    step = g * NJC + jc
    slot = lax.rem(step, 2)

    def page_copy(pg, j0, sl, p):
        return pltpu.make_async_copy(
            src_ref.at[pg, :, :, pl.ds(j0, JC), pl.ds(col0, cw)],
            xbuf.at[sl, :, :, :, p, :], sem.at[sl])

    def issue(st, sl):
        gg = st // NJC
        j0 = lax.rem(st, NJC) * JC

        def body(p, carry):
            page_copy(pidx_ref[gg * P + p], j0, sl, p).start()
            return carry
        lax.fori_loop(0, P, body, 0)

    @pl.when(step == 0)
    def _():
        issue(step, slot)

    @pl.when(step + 1 < G * NJC)
    def _():
        issue(step + 1, 1 - slot)

    def wait_body(p, carry):
        page_copy(0, 0, slot, p).wait()
        return carry
    lax.fori_loop(0, P, wait_body, 0)

    @pl.when(jc == 0)
    def _():
        acc_ref[...] = jnp.zeros(acc_ref.shape, F32)

    hw = cw // 2
    for c in range(2):
        tot = None
        for j in range(JC):
            parts = []
            for p2 in range(hw // LANES):
                col = c * hw + p2 * LANES
                for par in range(2):
                    lo = xbuf[slot, par, 0, j, :, col:col + LANES] + pos_ref[c, 0, j]
                    hi = xbuf[slot, par, 1, j, :, col:col + LANES] + pos_ref[c, 1, j]
                    parts.append(jnp.concatenate([lo, hi], axis=1))
            lhs = jnp.concatenate(parts, axis=0).astype(BF16)
            res = jnp.dot(lhs, w1_ref[c, j], preferred_element_type=F32)
            tot = res if tot is None else tot + res
        acc_ref[c] += tot

    @pl.when(jc == NJC - 1)
    def _():
        for c in range(2):
            hid = _silu(acc_ref[c] + b1_ref[c])
            comp = jnp.dot(hid.astype(BF16), w2_ref[c], preferred_element_type=F32) + b2_ref[c]
            for p2 in range(hw // LANES):
                for hl in range(2):
                    for par in range(2):
                        r0 = (p2 * 2 + par) * P
                        out_ref[c, 2 * p2 + hl, par] = (
                            comp[r0:r0 + P, hl * LANES:(hl + 1) * LANES].astype(BF16))


def _compress(page_idx, src, cmpw, l, *, G, P, col0, n_kv, head_dim):
    pos_e, w1_e, b1_e, w2_e, b2_e = cmpw
    half_rows = src.shape[3]
    JC = 8
    NJC = half_rows // JC
    cw = 2 * n_kv * head_dim
    nrow = 2 * (cw // 2 // LANES) * P
    grid_spec = pltpu.PrefetchScalarGridSpec(
        num_scalar_prefetch=1,
        grid=(G, NJC),
        in_specs=[
            pl.BlockSpec(memory_space=pl.ANY),
            pl.BlockSpec((None, 2, 2, JC, 1, LANES), lambda g, j, p: (l, 0, 0, j, 0, 0)),
            pl.BlockSpec((None, 2, JC, 2 * LANES, 2 * LANES), lambda g, j, p: (l, 0, j, 0, 0)),
            pl.BlockSpec((None, 2, 1, 2 * LANES), lambda g, j, p: (l, 0, 0, 0)),
            pl.BlockSpec((None, 2, 2 * LANES, 2 * LANES), lambda g, j, p: (l, 0, 0, 0)),
            pl.BlockSpec((None, 2, 1, 2 * LANES), lambda g, j, p: (l, 0, 0, 0)),
        ],
        out_specs=pl.BlockSpec((2, n_kv, 2, P, LANES), lambda g, j, p: (0, 0, 0, g, 0)),
        scratch_shapes=[
            pltpu.VMEM((2, 2, 2, JC, P, cw), F32),
            pltpu.VMEM((2, nrow, 2 * LANES), F32),
            pltpu.SemaphoreType.DMA((2,)),
        ],
    )
    return pl.pallas_call(
        functools.partial(_compress_body, P=P, JC=JC, NJC=NJC, G=G, col0=col0, cw=cw),
        grid_spec=grid_spec,
        out_shape=jax.ShapeDtypeStruct((2, n_kv, 2, G * P, LANES), BF16),
        compiler_params=_cparams(("arbitrary", "arbitrary")),
        name="compress",
    )(page_idx, src, pos_e, w1_e, b1_e, w2_e, b2_e)


def _blockdiag2(base):
    z = jnp.zeros_like(base)
    return jnp.concatenate([jnp.concatenate([base, z], axis=-1),
                            jnp.concatenate([z, base], axis=-1)], axis=-2)


def _compress_weights(cmp_pos, cmp_w1, cmp_b1, cmp_w2, cmp_b2):
    depth, _, blk, hd = cmp_pos.shape
    hidden = cmp_w1.shape[-1]
    hr = blk // 2
    pos_e = jnp.tile(cmp_pos.reshape(depth, 2, 2, hr, 1, hd), (1, 1, 1, 1, 1, 2))
    w1 = cmp_w1.reshape(depth, 2, 2, hr, hd, hidden).transpose(0, 1, 3, 2, 4, 5)
    w1_e = _blockdiag2(w1).reshape(depth, 2, hr, 4 * hd, 2 * hidden)
    b1_e = jnp.tile(cmp_b1.reshape(depth, 2, 1, hidden), (1, 1, 1, 2))
    w2_e = _blockdiag2(jnp.concatenate([cmp_w2, cmp_w2], axis=-1))
    b2_e = jnp.tile(cmp_b2.reshape(depth, 2, 1, hd), (1, 1, 1, 4))
    return pos_e, w1_e.astype(BF16), b1_e, w2_e.astype(BF16), b2_e


def _compress_cache_body(pidx_ref, src_ref, pos_ref, w1_ref, b1_ref, w2_ref, b2_ref, out_ref,
                         xbuf, acc_ref, sem, *, P, JC, NJC, G, n_kv):
    g = pl.program_id(0)
    jc = pl.program_id(1)
    step = g * NJC + jc
    slot = lax.rem(step, 2)
    nrow = n_kv * P

    def page_copy(pg, j0, sl, p):
        return pltpu.make_async_copy(src_ref.at[pg, pl.ds(0, 2), :, :, pl.ds(j0, JC)],
                                     xbuf.at[sl, :, :, :, :, p], sem.at[sl])

    def issue(st, sl):
        gg = st // NJC
        j0 = lax.rem(st, NJC) * JC

        def body(p, carry):
            page_copy(pidx_ref[gg * P + p], j0, sl, p).start()
            return carry
        lax.fori_loop(0, P, body, 0)

    @pl.when(step == 0)
    def _():
        issue(step, slot)

    @pl.when(step + 1 < G * NJC)
    def _():
        issue(step + 1, 1 - slot)

    def wait_body(p, carry):
        page_copy(0, 0, slot, p).wait()
        return carry
    lax.fori_loop(0, P, wait_body, 0)

    @pl.when(jc == 0)
    def _():
        acc_ref[...] = jnp.zeros(acc_ref.shape, F32)

    for c in range(2):
        tot = None
        for j in range(JC):
            lo = (xbuf[slot, c, :, 0, j] + pos_ref[c, 0, j]).reshape(nrow, LANES)
            hi = (xbuf[slot, c, :, 1, j] + pos_ref[c, 1, j]).reshape(nrow, LANES)
            lhs = jnp.concatenate([lo, hi], axis=1).astype(BF16)
            res = jnp.dot(lhs, w1_ref[c, j], preferred_element_type=F32)
            tot = res if tot is None else tot + res
        acc_ref[c] += tot

    @pl.when(jc == NJC - 1)
    def _():
        for c in range(2):
            hid = _silu(acc_ref[c] + b1_ref[c])
            comp = jnp.dot(hid.astype(BF16), w2_ref[c], preferred_element_type=F32) + b2_ref[c]
            for h in range(n_kv):
                for par in range(2):
                    out_ref[c, h, par] = comp[h * P:(h + 1) * P,
                                              par * LANES:(par + 1) * LANES].astype(BF16)


def _compress_cache(page_idx, src, cw, l, *, G, P):
    pos_t, w1_t, b1_e, w2_e, b2_e = cw
    n_kv, hh, page = src.shape[2], src.shape[4], src.shape[5]
    JC = 8
    NJC = hh // JC
    grid_spec = pltpu.PrefetchScalarGridSpec(
        num_scalar_prefetch=1,
        grid=(G, NJC),
        in_specs=[
            pl.BlockSpec(memory_space=pl.ANY),
            pl.BlockSpec((None, 2, 2, JC, 1, page), lambda g, j, p: (l, 0, 0, j, 0, 0)),
            pl.BlockSpec((None, 2, JC, 2 * page, 2 * LANES), lambda g, j, p: (l, 0, j, 0, 0)),
            pl.BlockSpec((None, 2, 1, 2 * LANES), lambda g, j, p: (l, 0, 0, 0)),
            pl.BlockSpec((None, 2, 2 * LANES, 2 * LANES), lambda g, j, p: (l, 0, 0, 0)),
            pl.BlockSpec((None, 2, 1, 2 * LANES), lambda g, j, p: (l, 0, 0, 0)),
        ],
        out_specs=pl.BlockSpec((2, n_kv, 2, P, LANES), lambda g, j, p: (0, 0, 0, g, 0)),
        scratch_shapes=[
            pltpu.VMEM((2, 2, n_kv, 2, JC, P, page), F32),
            pltpu.VMEM((2, n_kv * P, 2 * LANES), F32),
            pltpu.SemaphoreType.DMA((2,)),
        ],
    )
    return pl.pallas_call(
        functools.partial(_compress_cache_body, P=P, JC=JC, NJC=NJC, G=G, n_kv=n_kv),
        grid_spec=grid_spec,
        out_shape=jax.ShapeDtypeStruct((2, n_kv, 2, G * P, LANES), BF16),
        compiler_params=_cparams(("arbitrary", "arbitrary")),
        name="compress_cache",
    )(page_idx, src, pos_t, w1_t, b1_e, w2_e, b2_e)


def _compress_cache_weights(cmp_pos, cmp_w1, cmpw):
    depth, _, blk, hd = cmp_pos.shape
    hidden = cmp_w1.shape[-1]
    hh = hd // 2
    pos_t = jnp.tile(cmp_pos.reshape(depth, 2, blk, 2, hh).transpose(0, 1, 3, 4, 2)[:, :, :, :, None, :],
                     (1, 1, 1, 1, 1, 2))
    w1 = cmp_w1.reshape(depth, 2, blk, 2, hh, hidden).transpose(0, 1, 4, 3, 2, 5)
    w1_t = _blockdiag2(w1).reshape(depth, 2, hh, 4 * blk, 2 * hidden)
    _, _, b1_e, w2_e, b2_e = cmpw
    return pos_t, w1_t.astype(BF16), b1_e, w2_e, b2_e


def _stack_heads(cols, lo):
    rows = []
    for cj in cols:
        rows.append(jnp.where(lo, cj, 0.0))
        rows.append(jnp.where(lo, 0.0, cj))
    return jnp.concatenate(rows, axis=0).astype(BF16)


def _block_ids(n, pb):
    return jnp.where(n >= pb, 2 * (n - pb) + 1, 2 * n)


def _nsa_prompt_body(q_ref, z_ref, gt_ref, tc_ref, ta_ref, tb_ref, ks_ref, vs_ref, kw_ref, vw_ref,
                     kc_ref, vc_ref, o_ref, *, tq, nsub, **kw):
    for sub in range(nsub):
        rows = slice(sub * tq, (sub + 1) * tq)
        o_ref[rows, :] = _nsa_tile(
            (pl.program_id(2) * nsub + sub) * tq, q_ref[rows, :], z_ref[rows, :], gt_ref[rows, :],
            tc_ref[rows, :], ta_ref[rows, :], tb_ref[rows, :], ks_ref, vs_ref, kw_ref, vw_ref,
            kc_ref, vc_ref, tq=tq, **kw).astype(o_ref.dtype)


def _nsa_tile(t0, q_in, z_in, g, tc, ta, tb, ks_ref, vs_ref, kw_ref, vw_ref, kc_ref, vc_ref,
              *, tq, tk, T, PB, blk, half, scale):
    nb = 2 * PB
    shift = blk.bit_length() - 1
    lo = lax.broadcasted_iota(jnp.int32, (tq, LANES), 1) < LANES // 2
    q = q_in * scale
    q0, q1 = q[:, 0:LANES], q[:, LANES:2 * LANES]
    qc = _stack_heads([q0, q1], lo)
    qr = _stack_heads([_rope(q0, tc, ta, tb, half), _rope(q1, tc, ta, tb, half)], lo)
    tpos = t0 + lax.broadcasted_iota(jnp.int32, (tq, 1), 0)
    cur = lax.shift_right_logical(tpos, shift)

    def tile4(x):
        return jnp.concatenate([x, x, x, x], axis=0)

    zrow = jnp.zeros((LANES - nb, LANES), BF16)
    kc = jnp.concatenate([kc_ref[0], kc_ref[1], zrow], axis=0)
    vc = jnp.concatenate([vc_ref[0], vc_ref[1], zrow], axis=0)
    bid = _block_ids(lax.broadcasted_iota(jnp.int32, (nb, tq), 0), PB)
    tpos_l = t0 + lax.broadcasted_iota(jnp.int32, (nb, tq), 1)
    cur_l = lax.shift_right_logical(tpos_l, shift)
    cmask = (bid + 1) * blk - 1 <= tpos_l

    def lanes4(x):
        return jnp.concatenate([x, x, x, x], axis=1)

    cm4 = lanes4(cmask.astype(F32)) > 0.5
    sc = lax.dot_general(kc, qc, NT_DIMS, preferred_element_type=F32)[0:nb]
    sc = jnp.where(cm4, sc, NEG)
    mc = jnp.max(sc, axis=0, keepdims=True)
    ec = jnp.where(cm4, jnp.exp(sc - mc), 0.0)
    pct = ec / jnp.maximum(jnp.sum(ec, axis=0, keepdims=True), 1e-30)
    imp = pct[:, 0:tq] + pct[:, tq:2 * tq] + pct[:, 2 * tq:3 * tq] + pct[:, 3 * tq:4 * tq]

    forced = (bid == 0) | (bid == cur_l) | (bid == cur_l - 1)
    valid = bid <= cur_l
    score = jnp.where(valid, jnp.where(forced, BIG, imp), -BIG)
    def ranks():
        cnt = jnp.zeros((nb, tq), F32)
        for i in range(nb):
            other = score[i:i + 1, :]
            bi = 2 * (i % PB) + i // PB
            beats = (other > score) | ((other == score) & (bid > bi))
            cnt = cnt + beats.astype(F32)
        return cnt

    cnt = lax.cond(t0 + tq > TOP_N * blk, ranks, lambda: jnp.zeros((nb, tq), F32))
    sel_t = ((cnt < TOP_N) & valid).astype(F32)

    zpad = jnp.zeros((LANES - nb, tq), F32)
    pc = jnp.concatenate(
        [jnp.transpose(jnp.concatenate([pct[:, hh * tq:(hh + 1) * tq], zpad], axis=0))
         for hh in range(4)], axis=0)
    oc = jnp.dot(pc.astype(BF16), vc, preferred_element_type=F32)
    sel = jnp.transpose(jnp.concatenate([sel_t, zpad], axis=0))

    sel = sel.astype(BF16)
    row_l = lax.broadcasted_iota(jnp.int32, (LANES, tk), 0)
    bid_col = jnp.where(row_l < nb, _block_ids(row_l, PB), -1)
    key_l = lax.broadcasted_iota(jnp.int32, (LANES, tk), 1)
    key_q = lax.broadcasted_iota(jnp.int32, (tq, tk), 1)

    def sel_chunk(ci, carry):
        m, l, acc = carry
        k0 = pl.multiple_of(ci * tk, tk)
        s = lax.dot_general(qr, ks_ref[pl.ds(k0, tk), :], NT_DIMS, preferred_element_type=F32)
        expand = (bid_col == lax.shift_right_logical(key_l + k0, shift)).astype(BF16)
        bm = jnp.dot(sel, expand, preferred_element_type=F32)
        ok = (bm > 0.5) & (key_q + k0 <= tpos)
        s = s + tile4(jnp.where(ok, 0.0, NEG))
        m_new = jnp.maximum(m, jnp.max(s, axis=-1, keepdims=True))
        alpha = jnp.exp(m - m_new)
        p = jnp.exp(s - m_new)
        l = alpha * l + jnp.sum(p, axis=-1, keepdims=True)
        acc = alpha * acc + jnp.dot(p.astype(BF16), vs_ref[pl.ds(k0, tk), :],
                                    preferred_element_type=F32)
        return m_new, l, acc

    nchunks = (t0 + tq + tk - 1) // tk
    init = (jnp.full((4 * tq, 1), NEG, F32), jnp.zeros((4 * tq, 1), F32),
            jnp.zeros((4 * tq, LANES), F32))
    _, ls, accs = lax.fori_loop(0, nchunks, sel_chunk, init)
    osel = accs / ls

    ww = min(WINDOW + tq, T)
    ws = pl.multiple_of(jnp.clip(t0 - WINDOW, 0, T - ww), tq)
    sw = lax.dot_general(qr, kw_ref[pl.ds(ws, ww), :], NT_DIMS, preferred_element_type=F32)
    dl = tpos - (ws + lax.broadcasted_iota(jnp.int32, (tq, ww), 1))
    sw = sw + tile4(jnp.where((dl >= 0) & (dl < WINDOW), 0.0, NEG))
    pw = jnp.exp(sw - jnp.max(sw, axis=-1, keepdims=True))
    ow = jnp.dot(pw.astype(BF16), vw_ref[pl.ds(ws, ww), :], preferred_element_type=F32)
    ow = ow / jnp.sum(pw, axis=-1, keepdims=True)

    def gcol(br):
        return jnp.concatenate([g[:, 3 * hh + br:3 * hh + br + 1] for hh in range(4)], axis=0)

    o = gcol(0) * oc + gcol(1) * osel + gcol(2) * ow
    p0 = jnp.where(lo, o[0:tq], o[tq:2 * tq])
    p1 = jnp.where(lo, o[2 * tq:3 * tq], o[3 * tq:4 * tq])
    return jnp.concatenate([p0, p1], axis=1) * _silu(z_in)


def _nsa_prompt(proj, gates, tabs, ks, vs, kw, vw, cmp_e, *, B, T, n_kv, head_dim, q_col, z_col,
                blk, tq, tk):
    nsub = 2 if T % (2 * tq) == 0 else 1
    ts = nsub * tq
    nq = T // ts
    gw = 2 * LANES
    PB = T // (2 * blk)
    assert tq == LANES and 2 * PB <= LANES and tk % tq == 0
    row = lambda b, h, i: b * nq + i
    tab = pl.BlockSpec((ts, LANES), lambda b, h, i: (i, 0))
    kvs = pl.BlockSpec((None, None, T, LANES), lambda b, h, i: (b, h, 0, 0))
    return pl.pallas_call(
        functools.partial(_nsa_prompt_body, tq=tq, nsub=nsub, tk=tk, T=T, PB=PB, blk=blk,
                          half=head_dim // 8, scale=head_dim ** -0.5),
        grid=(B, n_kv, nq),
        in_specs=[
            pl.BlockSpec((ts, gw), lambda b, h, i: (row(b, h, i), q_col // gw + h)),
            pl.BlockSpec((ts, gw), lambda b, h, i: (row(b, h, i), z_col // gw + h)),
            pl.BlockSpec((None, None, ts, LANES), lambda b, h, i: (b, h, i, 0)),
            tab, tab, tab,
            kvs, kvs, kvs, kvs,
            pl.BlockSpec((None, None, 2, PB, LANES), lambda b, h, i: (0, h, 0, b, 0)),
            pl.BlockSpec((None, None, 2, PB, LANES), lambda b, h, i: (1, h, 0, b, 0)),
        ],
        out_specs=pl.BlockSpec((ts, gw), lambda b, h, i: (row(b, h, i), h)),
        out_shape=jax.ShapeDtypeStruct((B * T, n_kv * gw), BF16),
        compiler_params=_cparams(("parallel", "parallel", "arbitrary")),
        name="nsa_prompt",
    )(proj, proj, gates, *tabs, ks, vs, kw, vw, cmp_e, cmp_e)


def _outproj_body(x_ref, a_ref, b_ref, wa_ref, wb_ref, g_ref, o_ref):
    acc = jnp.dot(a_ref[...].astype(BF16), wa_ref[...], preferred_element_type=F32)
    acc = acc + jnp.dot(b_ref[...].astype(BF16), wb_ref[...], preferred_element_type=F32)
    ms = jnp.mean(acc * acc, axis=-1, keepdims=True)
    o_ref[...] = x_ref[...] + (acc * lax.rsqrt(ms + EPS)) * g_ref[...]


def _outproj(x, a, b, w_out, g, l, *, tm):
    M, D = x.shape
    ka, kb = a.shape[1], b.shape[1]
    return pl.pallas_call(
        _outproj_body,
        grid=(M // tm,),
        in_specs=[
            pl.BlockSpec((tm, D), lambda i: (i, 0)),
            pl.BlockSpec((tm, ka), lambda i: (i, 0)),
            pl.BlockSpec((tm, kb), lambda i: (i, 0)),
            pl.BlockSpec((None, ka, D), lambda i: (l, 0, 0)),
            pl.BlockSpec((None, kb, D), lambda i: (l, ka // kb, 0)),
            pl.BlockSpec((None, 1, D), lambda i: (l, 0, 0)),
        ],
        out_specs=pl.BlockSpec((tm, D), lambda i: (i, 0)),
        out_shape=jax.ShapeDtypeStruct((M, D), F32),
        compiler_params=_cparams(("parallel",)),
        name="outproj",
    )(x, a, b, w_out, w_out, g)


def _sample_prep_body(p_ref, st_ref, w_ref, cb_ref, lg_ref, lb_ref, gb_ref,
                      tc_ref, ta_ref, tb_ref,
                      y_ref, cn_ref, kvn_ref, win_ref, qc_ref, qr_ref, gt_ref, zs_ref,
                      *, C, HD, kvw, half, scale):
    nb = p_ref.shape[0]
    taps = w_ref.shape[0]
    tc, ta, tb = tc_ref[...], ta_ref[...], tb_ref[...]
    glu = p_ref[:, 0:C] * jax.nn.sigmoid(p_ref[:, C:2 * C])
    c = w_ref[taps - 1:taps, :] * glu + cb_ref[...]
    for k in range(taps - 1):
        c = c + w_ref[k:k + 1, :] * st_ref[k]
    cn_ref[0:taps - 2] = st_ref[1:taps - 1]
    cn_ref[taps - 2] = glu
    mu = jnp.mean(c, axis=-1, keepdims=True)
    d = c - mu
    var = jnp.mean(d * d, axis=-1, keepdims=True)
    yl = (d * lax.rsqrt(var + EPS)) * lg_ref[...] + lb_ref[...]
    y_ref[...] = _silu(yl) * _silu(p_ref[:, 2 * C:3 * C])

    q_col = 3 * C
    z_col = q_col + HD
    kv_col = z_col + HD
    for j in range(HD // LANES):
        qj = p_ref[:, q_col + j * LANES:q_col + (j + 1) * LANES] * scale
        qc_ref[:, j * LANES:(j + 1) * LANES] = qj
        qr_ref[:, j * LANES:(j + 1) * LANES] = _rope(qj, tc, ta, tb, half)
    zs_ref[...] = _silu(p_ref[:, z_col:z_col + HD])
    kvn_ref[:, 0:2 * kvw] = p_ref[:, kv_col:kv_col + 2 * kvw]
    kvn_ref[:, 3 * kvw:4 * kvw] = p_ref[:, kv_col + 3 * kvw:kv_col + 4 * kvw]
    win_ref[:, kvw:2 * kvw] = p_ref[:, kv_col + 5 * kvw:kv_col + 6 * kvw]
    for j in range(kvw // LANES):
        s0 = kv_col + 2 * kvw + j * LANES
        kvn_ref[:, 2 * kvw + j * LANES:2 * kvw + (j + 1) * LANES] = _rope(
            p_ref[:, s0:s0 + LANES], tc, ta, tb, half)
        w0 = kv_col + 4 * kvw + j * LANES
        win_ref[:, j * LANES:(j + 1) * LANES] = _rope(p_ref[:, w0:w0 + LANES], tc, ta, tb, half)
    gl_col = kv_col + 6 * kvw
    gt_ref[...] = jax.nn.sigmoid(p_ref[:, gl_col:gl_col + LANES] + gb_ref[...])


def _sample_prep(proj, state_conv, conv_w, conv_b, ln_g, ln_b, gate_b, tabs, l,
                 *, C, HD, kvw, head_dim):
    nb = proj.shape[0]
    taps = conv_w.shape[1]
    whole = lambda shape: pl.BlockSpec(shape, lambda i: (0,) * len(shape))
    vecc = pl.BlockSpec((None, 1, C), lambda i: (l, 0, 0))
    tab = whole((1, LANES))
    shapes = [
        ((nb, C), F32), ((taps - 1, nb, C), F32), ((nb, 4 * kvw), F32), ((nb, 2 * kvw), F32),
        ((nb, HD), F32), ((nb, HD), F32), ((nb, LANES), F32), ((nb, HD), F32),
    ]
    return pl.pallas_call(
        functools.partial(_sample_prep_body, C=C, HD=HD, kvw=kvw, half=head_dim // 8,
                          scale=head_dim ** -0.5),
        grid=(1,),
        in_specs=[
            whole(proj.shape),
            pl.BlockSpec((None, taps - 1, nb, C), lambda i: (l, 0, 0, 0)),
            pl.BlockSpec((None, taps, C), lambda i: (l, 0, 0)),
            vecc, vecc, vecc,
            pl.BlockSpec((None, 1, LANES), lambda i: (l, 0, 0)),
            tab, tab, tab,
        ],
        out_specs=[whole(s) for s, _ in shapes],
        out_shape=[jax.ShapeDtypeStruct(s, d) for s, d in shapes],
        compiler_params=_cparams(("arbitrary",)),
        name="sample_prep",
    )(proj, state_conv, conv_w, conv_b, ln_g, ln_b, gate_b, *tabs)


def _sample_cmp_body(qc_ref, kc_ref, vc_ref, oc_ref, idx_ref, *, P, n_kv):
    b = pl.program_id(0)
    nb = 2 * P
    qrow = qc_ref[pl.ds(b, 1), :]
    lo = lax.broadcasted_iota(jnp.int32, (1, LANES), 1) < LANES // 2
    zero4 = jnp.zeros((4, LANES), F32)
    imps = []
    for h in range(n_kv):
        c0 = qrow[:, 2 * h * LANES:(2 * h + 1) * LANES]
        c1 = qrow[:, (2 * h + 1) * LANES:(2 * h + 2) * LANES]
        qs = jnp.concatenate([jnp.where(lo, c0, 0.0), jnp.where(lo, 0.0, c0),
                              jnp.where(lo, c1, 0.0), jnp.where(lo, 0.0, c1), zero4],
                             axis=0).astype(BF16)
        kc = jnp.concatenate([kc_ref[h, 0], kc_ref[h, 1]], axis=0)
        vc = jnp.concatenate([vc_ref[h, 0], vc_ref[h, 1]], axis=0)
        s = lax.dot_general(qs, kc, NT_DIMS, preferred_element_type=F32)
        e = jnp.exp(s - jnp.max(s, axis=-1, keepdims=True))
        pc = e / jnp.maximum(jnp.sum(e, axis=-1, keepdims=True), 1e-30)
        oc_ref[h] = jnp.dot(pc.astype(BF16), vc, preferred_element_type=F32)
        imps.append(pc[0:1] + pc[1:2] + pc[2:3] + pc[3:4])
    imp = jnp.concatenate(imps, axis=0)
    bid = _block_ids(lax.broadcasted_iota(jnp.int32, (n_kv, nb), 1), P)
    forced = (bid == 0) | (bid == nb - 1)
    score = jnp.where(forced, BIG, imp)
    cnt = (score < BIG).astype(F32)
    for i in range(nb):
        col = score[:, i:i + 1]
        beats = (col > score) | ((col == score) & (bid > 2 * (i % P) + i // P))
        cnt = cnt + beats.astype(F32)
    lane = lax.broadcasted_iota(jnp.int32, (n_kv, LANES), 1)
    res = jnp.full((n_kv, LANES), -1.0, F32)
    bid1 = (bid + 1).astype(F32)
    for k in range(TOP_N):
        v = jnp.sum(jnp.where(cnt == k, bid1, 0.0), axis=-1, keepdims=True) - 1.0
        res = jnp.where(lane == k, v, res)
    idx_ref[...] = res.astype(jnp.int32)


def _sample_cmp(qc, cmp_e, *, DB, P, n_kv):
    ce = lambda c: pl.BlockSpec((None, n_kv, 2, P, LANES), lambda b: (c, 0, 0, b, 0))
    return pl.pallas_call(
        functools.partial(_sample_cmp_body, P=P, n_kv=n_kv),
        grid=(DB,),
        in_specs=[pl.BlockSpec(qc.shape, lambda b: (0, 0)), ce(0), ce(1)],
        out_specs=[pl.BlockSpec((None, n_kv, 8, LANES), lambda b: (b, 0, 0, 0)),
                   pl.BlockSpec((None, n_kv, LANES), lambda b: (b, 0, 0))],
        out_shape=[jax.ShapeDtypeStruct((DB, n_kv, 8, LANES), F32),
                   jax.ShapeDtypeStruct((DB, n_kv, LANES), jnp.int32)],
        compiler_params=_cparams(("parallel",)),
        name="sample_cmp",
    )(qc, cmp_e, cmp_e)


def _sample_attn_body(idx_ref, pages_ref, qr_ref, gt_ref, zs_ref, oc_ref, kvn_ref, wr_ref, cw_ref,
                      cache_ref, o_ref, wn_ref, selbuf, sem, *, P, n_kv, hd, kvw, nwin):
    b = pl.program_id(0)

    def page_copy(pg, h, slot):
        return pltpu.make_async_copy(cache_ref.at[pg, pl.ds(2, 2), h], selbuf.at[slot], sem.at[0])

    for h in range(n_kv):
        for k in range(TOP_N):
            ix = jnp.maximum(idx_ref[(b * n_kv + h) * LANES + k], 0)
            page_copy(pages_ref[b * P + lax.shift_right_logical(ix, 1)], h, h * TOP_N + k).start()

    newrow = wr_ref[pl.ds(b, 1), :]
    eye = (lax.broadcasted_iota(jnp.int32, (hd, hd), 0)
           == lax.broadcasted_iota(jnp.int32, (hd, hd), 1))
    last = lax.broadcasted_iota(jnp.int32, (hd, nwin), 1) == nwin - 1
    for c in range(2):
        for h in range(n_kv):
            v = newrow[:, c * kvw + h * hd:c * kvw + (h + 1) * hd]
            col = jnp.sum(jnp.where(eye, v, 0.0), axis=1, keepdims=True)
            wn_ref[c, h] = jnp.where(last, col, pltpu.roll(cw_ref[c, h], nwin - 1, 1))

    for s in range(n_kv * TOP_N):
        page_copy(0, 0, s).wait()

    qrow = qr_ref[pl.ds(b, 1), :]
    kvrow = kvn_ref[pl.ds(b, 1), :]
    g = gt_ref[pl.ds(b, 1), :]
    zero4 = jnp.zeros((4, hd), F32)
    zero41 = jnp.zeros((4, 1), F32)
    page = selbuf.shape[3]
    rows = page // 2
    nk = TOP_N * page
    key = lax.broadcasted_iota(jnp.int32, (1, nk), 1)
    slot_of_key = key // page
    half_of_key = (key // rows) & 1
    pieces = []
    for h in range(n_kv):
        qh = jnp.concatenate([qrow[:, (4 * h + gg) * hd:(4 * h + gg + 1) * hd] for gg in range(4)]
                             + [zero4], axis=0)
        qb = qh.astype(BF16)
        ks_new = kvrow[:, 2 * kvw + h * hd:2 * kvw + (h + 1) * hd]
        vs_new = kvrow[:, 3 * kvw + h * hd:3 * kvw + (h + 1) * hd]
        kt = jnp.concatenate([selbuf[h * TOP_N + k, 0] for k in range(TOP_N)], axis=1).astype(BF16)
        vt = jnp.concatenate([selbuf[h * TOP_N + k, 1] for k in range(TOP_N)], axis=1).astype(BF16)
        want = jnp.full((1, nk), -1, jnp.int32)
        for k in range(TOP_N):
            ix = idx_ref[(b * n_kv + h) * LANES + k]
            want = jnp.where(slot_of_key == k, jnp.where(ix >= 0, ix & 1, -1), want)
        s_all = jnp.dot(qb, kt, preferred_element_type=F32)
        s_all = jnp.where(half_of_key == want, s_all, NEG)
        s_new = jnp.sum(qb.astype(F32) * ks_new.astype(BF16).astype(F32), axis=-1, keepdims=True)
        m = jnp.maximum(jnp.max(s_all, axis=-1, keepdims=True), s_new)
        e = jnp.exp(s_all - m)
        e_new = jnp.exp(s_new - m)
        lsum = jnp.sum(e, axis=-1, keepdims=True) + e_new
        acc = e_new.astype(BF16).astype(F32) * vs_new.astype(BF16).astype(F32)
        acc = acc + lax.dot_general(e.astype(BF16), vt, NT_DIMS, preferred_element_type=F32)
        osel = acc / lsum

        sw = jnp.dot(qb, wn_ref[0, h].astype(BF16), preferred_element_type=F32)
        pw = jnp.exp(sw - jnp.max(sw, axis=-1, keepdims=True))
        ow = lax.dot_general(pw.astype(BF16), wn_ref[1, h].astype(BF16), NT_DIMS,
                             preferred_element_type=F32)
        ow = ow / jnp.sum(pw, axis=-1, keepdims=True)

        def gcol(br):
            return jnp.concatenate(
                [g[:, (4 * h + gg) * 3 + br:(4 * h + gg) * 3 + br + 1] for gg in range(4)]
                + [zero41], axis=0)

        o = gcol(0) * oc_ref[h][:, 0:hd] + gcol(1) * osel + gcol(2) * ow
        for gg in range(4):
            pieces.append(o[gg:gg + 1, :])
    o_ref[pl.ds(b, 1), :] = jnp.concatenate(pieces, axis=1) * zs_ref[pl.ds(b, 1), :]


def _sample_attn(idx, pages, qr, gates, zs, oc, kvn, winrow, cache_win_t, cache_t, l,
                 *, DB, P, n_kv, head_dim):
    kvw = n_kv * head_dim
    nwin = cache_win_t.shape[-1]
    page = cache_t.shape[-1]
    win_blk = (None, 2, n_kv, head_dim, nwin)
    whole = lambda a: pl.BlockSpec(a.shape, lambda b, i, p: (0,) * a.ndim)
    grid_spec = pltpu.PrefetchScalarGridSpec(
        num_scalar_prefetch=2,
        grid=(DB,),
        in_specs=[
            whole(qr), whole(gates), whole(zs),
            pl.BlockSpec((None, n_kv, 8, LANES), lambda b, i, p: (b, 0, 0, 0)),
            whole(kvn), whole(winrow),
            pl.BlockSpec(win_blk, lambda b, i, p: (l * DB + b, 0, 0, 0, 0)),
            pl.BlockSpec(memory_space=pl.ANY),
        ],
        out_specs=[
            pl.BlockSpec(qr.shape, lambda b, i, p: (0, 0)),
            pl.BlockSpec(win_blk, lambda b, i, p: (b, 0, 0, 0, 0)),
        ],
        scratch_shapes=[
            pltpu.VMEM((n_kv * TOP_N, 2, head_dim, page), F32),
            pltpu.SemaphoreType.DMA((1,)),
        ],
    )
    return pl.pallas_call(
        functools.partial(_sample_attn_body, P=P, n_kv=n_kv, hd=head_dim, kvw=kvw, nwin=nwin),
        grid_spec=grid_spec,
        out_shape=[jax.ShapeDtypeStruct(qr.shape, F32),
                   jax.ShapeDtypeStruct((DB, 2, n_kv, head_dim, nwin), F32)],
        compiler_params=_cparams(("arbitrary",)),
        name="sample_attn",
    )(idx, pages, qr, gates, zs, oc, kvn, winrow, cache_win_t, cache_t)


def kernel(x_prompt, x_sample, cache_kv_pages, cache_win, state_conv, page_table, w_in, w_out,
           norm_pre, norm_post, conv_w, conv_b, conv_ln_g, conv_ln_b, cmp_pos, cmp_w1, cmp_b1,
           cmp_w2, cmp_b2, gate_b):
    B, S, D = x_prompt.shape
    DB, dec_seq, _ = x_sample.shape
    depth, n_pool, page, _, n_kv, head_dim = cache_kv_pages.shape
    n_pages = page_table.shape[1]
    past_len = n_pages * page
    C = conv_w.shape[-1]
    taps = conv_w.shape[1]
    n_heads = gate_b.shape[-1] // 3
    HD = n_heads * head_dim
    kvw = n_kv * head_dim
    blk = cmp_pos.shape[2]
    n_main = 3 * C + 2 * HD + 6 * kvw
    nwin = cache_win.shape[2]
    gpg = 3 * (n_heads // n_kv)
    assert dec_seq == 1 and page == 2 * blk and past_len % blk == 0
    assert w_in.shape[-1] == n_main + 3 * n_heads and n_heads == 4 * n_kv and head_dim * 2 == LANES
    assert nwin == WINDOW and S >= WINDOW and past_len // blk >= TOP_N and S % page == 0

    w_out_b = w_out.astype(BF16)
    gate_bp = jnp.pad(gate_b, ((0, 0), (0, LANES - 3 * n_heads))).reshape(depth, 1, LANES)
    cmpw = _compress_weights(cmp_pos, cmp_w1, cmp_b1, cmp_w2, cmp_b2)
    cmpw_c = _compress_cache_weights(cmp_pos, cmp_w1, cmpw)
    vec3 = lambda v: v.reshape(depth, 1, v.shape[-1])
    norm_pre3, norm_post3 = vec3(norm_pre), vec3(norm_post)
    conv_b3, ln_g3, ln_b3 = vec3(conv_b), vec3(conv_ln_g), vec3(conv_ln_b)

    tabs_p = _rope_tables(jnp.arange(S, dtype=jnp.int32), head_dim)
    tabs_s = _rope_tables(jnp.full((1,), past_len, jnp.int32), head_dim)

    cache_t = cache_kv_pages.transpose(0, 1, 3, 4, 5, 2).reshape(depth * n_pool, 4, n_kv, head_dim, page)
    cache_t6 = cache_t.reshape(depth * n_pool, 4, n_kv, 2, head_dim // 2, page)
    cache_win_t = cache_win.transpose(0, 1, 3, 4, 5, 2).reshape(depth * DB, 2, n_kv, head_dim, nwin)
    state_t = state_conv.transpose(0, 2, 1, 3)
    w_in_t = w_in.transpose(0, 2, 1).astype(BF16)
    pages_prompt = jnp.arange(B * S // page, dtype=jnp.int32)
    conv_zero = jnp.zeros((B, taps - 1, C), F32)

    xp = x_prompt.reshape(B * S, D)
    xs = x_sample.reshape(DB, D)
    tm_in = 1024 if (B * S) % 1024 == 0 else 256
    tm_out = 512 if (B * S) % 512 == 0 else 256
    outs = [[] for _ in range(6)]
    for l in range(depth):
        proj = _inproj(xp, norm_pre3, w_in_t, l, tm=tm_in, tn=1024)
        yconv, conv_new = _conv_prompt(proj, conv_zero, conv_w, conv_b3, ln_g3, ln_b3, l,
                                       B=B, T=S, C=C, tt=256)
        kvn, winr, ks, vs, kw, vw, gates = _assemble_prompt(
            proj, gate_bp, tabs_p, l, B=B, T=S, n_kv=n_kv, head_dim=head_dim,
            kv_col=3 * C + 2 * HD, gl_col=n_main, tt=256, gpg=gpg, nwin=nwin)
        cmp_e = _compress(pages_prompt, proj.reshape(B * S // page, 2, 2, blk // 2, proj.shape[1]),
                          cmpw, l, G=1, P=B * S // page, col0=3 * C + 2 * HD, n_kv=n_kv,
                          head_dim=head_dim)
        attn = _nsa_prompt(proj, gates, tabs_p, ks, vs, kw, vw, cmp_e, B=B, T=S, n_kv=n_kv,
                           head_dim=head_dim, q_col=3 * C, z_col=3 * C + HD, blk=blk, tq=128, tk=512)
        xp = _outproj(xp, yconv, attn, w_out_b, norm_post3, l, tm=tm_out)
        outs[0].append(kvn.transpose(0, 4, 1, 2, 3))
        outs[2].append(winr.transpose(0, 4, 1, 2, 3))
        outs[4].append(conv_new)

        proj_s = _inproj(xs, norm_pre3, w_in_t, l, tm=DB, tn=1024)
        (yconv_s, conv_new_s, kvn_s, winrow_s, qc_s, qr_s, gates_s, zs_s) = _sample_prep(
            proj_s, state_t, conv_w, conv_b3, ln_g3, ln_b3, gate_bp, tabs_s, l,
            C=C, HD=HD, kvw=kvw, head_dim=head_dim)
        pages_s = (page_table + l * n_pool).reshape(-1).astype(jnp.int32)
        cmp_s = _compress_cache(pages_s, cache_t6, cmpw_c, l, G=DB, P=n_pages)
        oc_s, idx_s = _sample_cmp(qc_s, cmp_s, DB=DB, P=n_pages, n_kv=n_kv)
        attn_s, win_new_s = _sample_attn(idx_s.reshape(-1), pages_s, qr_s, gates_s, zs_s, oc_s, kvn_s,
                                         winrow_s, cache_win_t, cache_t, l,
                                         DB=DB, P=n_pages, n_kv=n_kv, head_dim=head_dim)
        xs = _outproj(xs, yconv_s, attn_s, w_out_b, norm_post3, l, tm=DB)
        outs[1].append(kvn_s.reshape(DB, 1, 4, n_kv, head_dim))
        outs[3].append(win_new_s.transpose(0, 4, 1, 2, 3))
        outs[5].append(conv_new_s.transpose(1, 0, 2))

    return (xp.reshape(B, S, D), xs.reshape(DB, 1, D), jnp.stack(outs[0]), jnp.stack(outs[1]),
            jnp.stack(outs[2]), jnp.stack(outs[3]), jnp.stack(outs[4]), jnp.stack(outs[5]))
```

```python
import functools

import jax
import jax.numpy as jnp
from jax import lax
from jax.experimental import pallas as pl
from jax.experimental.pallas import tpu as pltpu

F32 = jnp.float32
BF16 = jnp.bfloat16

TOP_N = 16
WINDOW = 512
ROPE_THETA = 500000.0
EPS = 1e-6
BIG = 1e9
NEG = -1e30
LANES = 128
VMEM_LIMIT = 56 * 1024 * 1024

NT_DIMS = (((1,), (1,)), ((), ()))


def _silu(x):
    return x * jax.nn.sigmoid(x)


def _cparams(sem):
    return pltpu.CompilerParams(dimension_semantics=sem, vmem_limit_bytes=VMEM_LIMIT)


def _inproj_body(x_ref, g_ref, w_ref, o_ref, u_ref, *, n_in, tn):
    j = pl.program_id(1)

    @pl.when(j == 0)
    def _():
        x = x_ref[...]
        ms = jnp.mean(x * x, axis=-1, keepdims=True)
        u_ref[...] = ((x * lax.rsqrt(ms + EPS)) * g_ref[...]).astype(u_ref.dtype)

    nj = pl.num_programs(1)

    @pl.when(j < nj - 1)
    def _():
        o_ref[...] = lax.dot_general(u_ref[...].astype(BF16), w_ref[...].astype(BF16), NT_DIMS,
                                     preferred_element_type=F32)

    @pl.when(j == nj - 1)
    def _():
        row = j * tn + lax.broadcasted_iota(jnp.int32, (tn, 1), 0)
        w = jnp.where(row < n_in, w_ref[...], 0.0).astype(BF16)
        o_ref[...] = lax.dot_general(u_ref[...].astype(BF16), w, NT_DIMS, preferred_element_type=F32)


def _inproj(x, g, w_in_t, l, *, tm, tn):
    M, D = x.shape
    n_in = w_in_t.shape[1]
    nj = pl.cdiv(n_in, tn)
    u_dtype = BF16 if tm % 16 == 0 else F32
    return pl.pallas_call(
        functools.partial(_inproj_body, n_in=n_in, tn=tn),
        grid=(M // tm, nj),
        in_specs=[
            pl.BlockSpec((tm, D), lambda i, j: (i, 0)),
            pl.BlockSpec((None, 1, D), lambda i, j: (l, 0, 0)),
            pl.BlockSpec((None, tn, D), lambda i, j: (l, j, 0)),
        ],
        out_specs=pl.BlockSpec((tm, tn), lambda i, j: (i, j)),
        out_shape=jax.ShapeDtypeStruct((M, nj * tn), F32),
        scratch_shapes=[pltpu.VMEM((tm, D), u_dtype)],
        compiler_params=_cparams(("parallel", "arbitrary")),
        name="inproj",
    )(x, g, w_in_t)


def _conv_body(av_ref, ag_ref, z_ref, buf_ref, w_ref, cb_ref, lg_ref, lb_ref,
               y_ref, cn_ref, ap_ref, zs_ref, cv_ref, *, tt, nt, taps, rc, cc):
    t = pl.program_id(1)
    hist = -(-(taps - 1) // 8) * 8
    off = hist - (taps - 1)
    C = ap_ref.shape[1]

    @pl.when(t == 0)
    def _():
        ap_ref[0:off, :] = jnp.zeros((off, C), F32)
        ap_ref[off:hist, :] = buf_ref[...]

    @pl.when(t > 0)
    def _():
        ap_ref[0:hist, :] = ap_ref[tt:tt + hist, :]

    ap_ref[hist:hist + tt, :] = av_ref[...] * jax.nn.sigmoid(ag_ref[...])

    nz = zs_ref.shape[1]
    for s in range(1, 8):
        zs_ref[s - 1] = ap_ref[s:s + nz, :]
    for r0 in range(0, tt, rc):
        for c0 in range(0, C, cc):
            acc = jnp.zeros((rc, cc), F32)
            for k in range(taps):
                s, a = (k + off) % 8, (k + off) // 8
                src = ap_ref if s == 0 else zs_ref.at[s - 1]
                acc = acc + w_ref[k:k + 1, c0:c0 + cc] * src[r0 + 8 * a:r0 + 8 * a + rc, c0:c0 + cc]
            cv_ref[r0:r0 + rc, c0:c0 + cc] = acc
        c = cv_ref[r0:r0 + rc, :] + cb_ref[...]
        mu = jnp.mean(c, axis=-1, keepdims=True)
        d = c - mu
        var = jnp.mean(d * d, axis=-1, keepdims=True)
        yl = (d * lax.rsqrt(var + EPS)) * lg_ref[...] + lb_ref[...]
        y_ref[r0:r0 + rc, :] = (_silu(yl) * _silu(z_ref[r0:r0 + rc, :])).astype(y_ref.dtype)

    @pl.when(t == nt - 1)
    def _():
        cn_ref[...] = ap_ref[tt + off:tt + hist, :]


def _conv_prompt(proj, conv_buf, conv_w, conv_b, ln_g, ln_b, l, *, B, T, C, tt):
    taps = conv_w.shape[1]
    nt = T // tt
    hist = -(-(taps - 1) // 8) * 8
    vec = pl.BlockSpec((None, 1, C), lambda b, t: (l, 0, 0))
    return pl.pallas_call(
        functools.partial(_conv_body, tt=tt, nt=nt, taps=taps, rc=32, cc=min(C, 4 * LANES)),
        grid=(B, nt),
        in_specs=[
            pl.BlockSpec((tt, C), lambda b, t: (b * nt + t, 0)),
            pl.BlockSpec((tt, C), lambda b, t: (b * nt + t, 1)),
            pl.BlockSpec((tt, C), lambda b, t: (b * nt + t, 2)),
            pl.BlockSpec((None, taps - 1, C), lambda b, t: (b, 0, 0)),
            pl.BlockSpec((None, taps, C), lambda b, t: (l, 0, 0)),
            vec, vec, vec,
        ],
        out_specs=[
            pl.BlockSpec((tt, C), lambda b, t: (b * nt + t, 0)),
            pl.BlockSpec((None, taps - 1, C), lambda b, t: (b, 0, 0)),
        ],
        out_shape=[jax.ShapeDtypeStruct((B * T, C), BF16),
                   jax.ShapeDtypeStruct((B, taps - 1, C), F32)],
        scratch_shapes=[pltpu.VMEM((tt + hist, C), F32), pltpu.VMEM((7, tt + hist - 8, C), F32),
                        pltpu.VMEM((tt, C), F32)],
        compiler_params=_cparams(("parallel", "arbitrary")),
        name="conv_prompt",
    )(proj, proj, proj, conv_buf, conv_w, conv_b, ln_g, ln_b)


def _rope_tables(pos, head_dim):
    rope_dim = head_dim // 4
    half = rope_dim // 2
    inv = ROPE_THETA ** (-jnp.arange(half, dtype=F32) / half)
    ang = pos.astype(F32)[:, None] * inv[None, :]
    cos, sin = jnp.cos(ang), jnp.sin(ang)
    d = jnp.arange(LANES) % head_dim
    f = d % half
    cosl, sinl = cos[:, f], sin[:, f]
    c = jnp.where(d < rope_dim, cosl, 1.0)
    a = jnp.where(d < half, -sinl, 0.0)
    b = jnp.where((d >= half) & (d < rope_dim), sinl, 0.0)
    return c.astype(F32), a.astype(F32), b.astype(F32)


def _rope(x, c, a, b, half):
    return x * c + pltpu.roll(x, LANES - half, 1) * a + pltpu.roll(x, half, 1) * b


def _dup_head(x, odd, lo):
    r = pltpu.roll(x, LANES // 2, 1)
    return jnp.where(lo, r, x) if odd else jnp.where(lo, x, r)


def _assemble_body(a_ref, b_ref, c_ref, gl_ref, gb_ref, tc_ref, ta_ref, tb_ref,
                   kvn_ref, win_ref, ks_ref, vs_ref, kw_ref, vw_ref, gt_ref, *, n_kv, half, gpg):
    tc, ta, tb = tc_ref[...], ta_ref[...], tb_ref[...]
    tt = tc.shape[0]
    lo = lax.broadcasted_iota(jnp.int32, (tt, LANES), 1) < LANES // 2
    kvw = n_kv * LANES // 2
    ncol = kvw // LANES

    hd = LANES // 2

    def put_t(ref, kind, j, x):
        xt = jnp.transpose(x)
        ref[kind, 2 * j] = xt[0:hd]
        ref[kind, 2 * j + 1] = xt[hd:2 * hd]

    a = a_ref[...]
    b = b_ref[...]
    c = c_ref[...]
    ksr, kwr = [], []
    for j in range(ncol):
        ksr.append(_rope(b[:, j * LANES:(j + 1) * LANES], tc, ta, tb, half))
        kwr.append(_rope(c[:, j * LANES:(j + 1) * LANES], tc, ta, tb, half))
        put_t(kvn_ref, 0, j, a[:, j * LANES:(j + 1) * LANES])
        put_t(kvn_ref, 1, j, a[:, kvw + j * LANES:kvw + (j + 1) * LANES])
        put_t(kvn_ref, 2, j, ksr[j])
        put_t(kvn_ref, 3, j, b[:, kvw + j * LANES:kvw + (j + 1) * LANES])
        put_t(win_ref, 0, j, kwr[j])
        put_t(win_ref, 1, j, c[:, kvw + j * LANES:kvw + (j + 1) * LANES])

    g = jax.nn.sigmoid(gl_ref[...] + gb_ref[...])
    for h in range(n_kv):
        j, odd = h // 2, h % 2
        ks_ref[h] = _dup_head(ksr[j], odd, lo).astype(BF16)
        vs_ref[h] = _dup_head(b[:, kvw + j * LANES:kvw + (j + 1) * LANES], odd, lo).astype(BF16)
        kw_ref[h] = _dup_head(kwr[j], odd, lo).astype(BF16)
        vw_ref[h] = _dup_head(c[:, kvw + j * LANES:kvw + (j + 1) * LANES], odd, lo).astype(BF16)
        gt_ref[h] = g if h == 0 else pltpu.roll(g, LANES - gpg * h, 1)


def _assemble_prompt(proj, gate_b, tabs, l, *, B, T, n_kv, head_dim, kv_col, gl_col, tt, gpg, nwin):
    kvw = n_kv * head_dim
    assert nwin % tt == 0 and T >= nwin
    nt = T // tt
    cb = kv_col // (2 * kvw)
    row = lambda b, t: b * nt + t
    tab = pl.BlockSpec((tt, LANES), lambda b, t: (t, 0))
    exp = pl.BlockSpec((None, n_kv, tt, LANES), lambda b, t: (b, 0, t, 0))
    exp_shape = jax.ShapeDtypeStruct((B, n_kv, T, LANES), BF16)
    return pl.pallas_call(
        functools.partial(_assemble_body, n_kv=n_kv, half=head_dim // 8, gpg=gpg),
        grid=(B, nt),
        in_specs=[
            pl.BlockSpec((tt, 2 * kvw), lambda b, t: (row(b, t), cb)),
            pl.BlockSpec((tt, 2 * kvw), lambda b, t: (row(b, t), cb + 1)),
            pl.BlockSpec((tt, 2 * kvw), lambda b, t: (row(b, t), cb + 2)),
            pl.BlockSpec((tt, LANES), lambda b, t: (row(b, t), gl_col // LANES)),
            pl.BlockSpec((None, 1, LANES), lambda b, t: (l, 0, 0)),
            tab, tab, tab,
        ],
        out_specs=[
            pl.BlockSpec((None, 4, n_kv, head_dim, tt), lambda b, t: (b, 0, 0, 0, t)),
            pl.BlockSpec((None, 2, n_kv, head_dim, tt),
                         lambda b, t: (b, 0, 0, 0, jnp.maximum(t - (nt - nwin // tt), 0))),
            exp, exp, exp, exp,
            pl.BlockSpec((None, n_kv, tt, LANES), lambda b, t: (b, 0, t, 0)),
        ],
        out_shape=[
            jax.ShapeDtypeStruct((B, 4, n_kv, head_dim, T), F32),
            jax.ShapeDtypeStruct((B, 2, n_kv, head_dim, nwin), F32),
            exp_shape, exp_shape, exp_shape, exp_shape,
            jax.ShapeDtypeStruct((B, n_kv, T, LANES), F32),
        ],
        compiler_params=_cparams(("parallel", "arbitrary")),
        name="assemble_prompt",
    )(proj, proj, proj, proj, gate_b, *tabs)


def _compress_body(pidx_ref, src_ref, pos_ref, w1_ref, b1_ref, w2_ref, b2_ref, out_ref,
                   xbuf, acc_ref, sem, *, P, JC, NJC, G, col0, cw):
    g = pl.program_id(0)
    jc = pl.program_id(1)
    step = g * NJC + jc
    slot = lax.rem(step, 2)

    def page_copy(pg, j0, sl, p):
        return pltpu.make_async_copy(
            src_ref.at[pg, :, :, pl.ds(j0, JC), pl.ds(col0, cw)],
            xbuf.at[sl, :, :, :, p, :], sem.at[sl])

    def issue(st, sl):
        gg = st // NJC
        j0 = lax.rem(st, NJC) * JC

        def body(p, carry):
            page_copy(pidx_ref[gg * P + p], j0, sl, p).start()
            return carry
        lax.fori_loop(0, P, body, 0)

    @pl.when(step == 0)
    def _():
        issue(step, slot)

    @pl.when(step + 1 < G * NJC)
    def _():
        issue(step + 1, 1 - slot)

    def wait_body(p, carry):
        page_copy(0, 0, slot, p).wait()
        return carry
    lax.fori_loop(0, P, wait_body, 0)

    @pl.when(jc == 0)
    def _():
        acc_ref[...] = jnp.zeros(acc_ref.shape, F32)

    hw = cw // 2
    for c in range(2):
        tot = None
        for j in range(JC):
            parts = []
            for p2 in range(hw // LANES):
                col = c * hw + p2 * LANES
                for par in range(2):
                    lo = xbuf[slot, par, 0, j, :, col:col + LANES] + pos_ref[c, 0, j]
                    hi = xbuf[slot, par, 1, j, :, col:col + LANES] + pos_ref[c, 1, j]
                    parts.append(jnp.concatenate([lo, hi], axis=1))
            lhs = jnp.concatenate(parts, axis=0).astype(BF16)
            res = jnp.dot(lhs, w1_ref[c, j], preferred_element_type=F32)
            tot = res if tot is None else tot + res
        acc_ref[c] += tot

    @pl.when(jc == NJC - 1)
    def _():
        for c in range(2):
            hid = _silu(acc_ref[c] + b1_ref[c])
            comp = jnp.dot(hid.astype(BF16), w2_ref[c], preferred_element_type=F32) + b2_ref[c]
            for p2 in range(hw // LANES):
                for hl in range(2):
                    for par in range(2):
                        r0 = (p2 * 2 + par) * P
                        out_ref[c, 2 * p2 + hl, par] = (
                            comp[r0:r0 + P, hl * LANES:(hl + 1) * LANES].astype(BF16))


def _compress(page_idx, src, cmpw, l, *, G, P, col0, n_kv, head_dim):
    pos_e, w1_e, b1_e, w2_e, b2_e = cmpw
    half_rows = src.shape[3]
    JC = 8
    NJC = half_rows // JC
    cw = 2 * n_kv * head_dim
    nrow = 2 * (cw // 2 // LANES) * P
    grid_spec = pltpu.PrefetchScalarGridSpec(
        num_scalar_prefetch=1,
        grid=(G, NJC),
        in_specs=[
            pl.BlockSpec(memory_space=pl.ANY),
            pl.BlockSpec((None, 2, 2, JC, 1, LANES), lambda g, j, p: (l, 0, 0, j, 0, 0)),
            pl.BlockSpec((None, 2, JC, 2 * LANES, 2 * LANES), lambda g, j, p: (l, 0, j, 0, 0)),
            pl.BlockSpec((None, 2, 1, 2 * LANES), lambda g, j, p: (l, 0, 0, 0)),
            pl.BlockSpec((None, 2, 2 * LANES, 2 * LANES), lambda g, j, p: (l, 0, 0, 0)),
            pl.BlockSpec((None, 2, 1, 2 * LANES), lambda g, j, p: (l, 0, 0, 0)),
        ],
        out_specs=pl.BlockSpec((2, n_kv, 2, P, LANES), lambda g, j, p: (0, 0, 0, g, 0)),
        scratch_shapes=[
            pltpu.VMEM((2, 2, 2, JC, P, cw), F32),
            pltpu.VMEM((2, nrow, 2 * LANES), F32),
            pltpu.SemaphoreType.DMA((2,)),
        ],
    )
    return pl.pallas_call(
        functools.partial(_compress_body, P=P, JC=JC, NJC=NJC, G=G, col0=col0, cw=cw),
        grid_spec=grid_spec,
        out_shape=jax.ShapeDtypeStruct((2, n_kv, 2, G * P, LANES), BF16),
        compiler_params=_cparams(("arbitrary", "arbitrary")),
        name="compress",
    )(page_idx, src, pos_e, w1_e, b1_e, w2_e, b2_e)


def _blockdiag2(base):
    z = jnp.zeros_like(base)
    return jnp.concatenate([jnp.concatenate([base, z], axis=-1),
                            jnp.concatenate([z, base], axis=-1)], axis=-2)


def _compress_weights(cmp_pos, cmp_w1, cmp_b1, cmp_w2, cmp_b2):
    depth, _, blk, hd = cmp_pos.shape
    hidden = cmp_w1.shape[-1]
    hr = blk // 2
    pos_e = jnp.tile(cmp_pos.reshape(depth, 2, 2, hr, 1, hd), (1, 1, 1, 1, 1, 2))
    w1 = cmp_w1.reshape(depth, 2, 2, hr, hd, hidden).transpose(0, 1, 3, 2, 4, 5)
    w1_e = _blockdiag2(w1).reshape(depth, 2, hr, 4 * hd, 2 * hidden)
    b1_e = jnp.tile(cmp_b1.reshape(depth, 2, 1, hidden), (1, 1, 1, 2))
    w2_e = _blockdiag2(jnp.concatenate([cmp_w2, cmp_w2], axis=-1))
    b2_e = jnp.tile(cmp_b2.reshape(depth, 2, 1, hd), (1, 1, 1, 4))
    return pos_e, w1_e.astype(BF16), b1_e, w2_e.astype(BF16), b2_e


def _compress_cache_body(pidx_ref, src_ref, pos_ref, w1_ref, b1_ref, w2_ref, b2_ref, out_ref,
                         xbuf, acc_ref, sem, *, P, JC, NJC, G, n_kv):
    g = pl.program_id(0)
    jc = pl.program_id(1)
    step = g * NJC + jc
    slot = lax.rem(step, 2)
    nrow = n_kv * P

    def page_copy(pg, j0, sl, p):
        return pltpu.make_async_copy(src_ref.at[pg, pl.ds(0, 2), :, :, pl.ds(j0, JC)],
                                     xbuf.at[sl, :, :, :, :, p], sem.at[sl])

    def issue(st, sl):
        gg = st // NJC
        j0 = lax.rem(st, NJC) * JC

        def body(p, carry):
            page_copy(pidx_ref[gg * P + p], j0, sl, p).start()
            return carry
        lax.fori_loop(0, P, body, 0)

    @pl.when(step == 0)
    def _():
        issue(step, slot)

    @pl.when(step + 1 < G * NJC)
    def _():
        issue(step + 1, 1 - slot)

    def wait_body(p, carry):
        page_copy(0, 0, slot, p).wait()
        return carry
    lax.fori_loop(0, P, wait_body, 0)

    @pl.when(jc == 0)
    def _():
        acc_ref[...] = jnp.zeros(acc_ref.shape, F32)

    for c in range(2):
        tot = None
        for j in range(JC):
            lo = (xbuf[slot, c, :, 0, j] + pos_ref[c, 0, j]).reshape(nrow, LANES)
            hi = (xbuf[slot, c, :, 1, j] + pos_ref[c, 1, j]).reshape(nrow, LANES)
            lhs = jnp.concatenate([lo, hi], axis=1).astype(BF16)
            res = jnp.dot(lhs, w1_ref[c, j], preferred_element_type=F32)
            tot = res if tot is None else tot + res
        acc_ref[c] += tot

    @pl.when(jc == NJC - 1)
    def _():
        for c in range(2):
            hid = _silu(acc_ref[c] + b1_ref[c])
            comp = jnp.dot(hid.astype(BF16), w2_ref[c], preferred_element_type=F32) + b2_ref[c]
            for h in range(n_kv):
                for par in range(2):
                    out_ref[c, h, par] = comp[h * P:(h + 1) * P,
                                              par * LANES:(par + 1) * LANES].astype(BF16)


def _compress_cache(page_idx, src, cw, l, *, G, P):
    pos_t, w1_t, b1_e, w2_e, b2_e = cw
    n_kv, hh, page = src.shape[2], src.shape[4], src.shape[5]
    JC = 8
    NJC = hh // JC
    grid_spec = pltpu.PrefetchScalarGridSpec(
        num_scalar_prefetch=1,
        grid=(G, NJC),
        in_specs=[
            pl.BlockSpec(memory_space=pl.ANY),
            pl.BlockSpec((None, 2, 2, JC, 1, page), lambda g, j, p: (l, 0, 0, j, 0, 0)),
            pl.BlockSpec((None, 2, JC, 2 * page, 2 * LANES), lambda g, j, p: (l, 0, j, 0, 0)),
            pl.BlockSpec((None, 2, 1, 2 * LANES), lambda g, j, p: (l, 0, 0, 0)),
            pl.BlockSpec((None, 2, 2 * LANES, 2 * LANES), lambda g, j, p: (l, 0, 0, 0)),
            pl.BlockSpec((None, 2, 1, 2 * LANES), lambda g, j, p: (l, 0, 0, 0)),
        ],
        out_specs=pl.BlockSpec((2, n_kv, 2, P, LANES), lambda g, j, p: (0, 0, 0, g, 0)),
        scratch_shapes=[
            pltpu.VMEM((2, 2, n_kv, 2, JC, P, page), F32),
            pltpu.VMEM((2, n_kv * P, 2 * LANES), F32),
            pltpu.SemaphoreType.DMA((2,)),
        ],
    )
    return pl.pallas_call(
        functools.partial(_compress_cache_body, P=P, JC=JC, NJC=NJC, G=G, n_kv=n_kv),
        grid_spec=grid_spec,
        out_shape=jax.ShapeDtypeStruct((2, n_kv, 2, G * P, LANES), BF16),
        compiler_params=_cparams(("arbitrary", "arbitrary")),
        name="compress_cache",
    )(page_idx, src, pos_t, w1_t, b1_e, w2_e, b2_e)


def _compress_cache_weights(cmp_pos, cmp_w1, cmpw):
    depth, _, blk, hd = cmp_pos.shape
    hidden = cmp_w1.shape[-1]
    hh = hd // 2
    pos_t = jnp.tile(cmp_pos.reshape(depth, 2, blk, 2, hh).transpose(0, 1, 3, 4, 2)[:, :, :, :, None, :],
                     (1, 1, 1, 1, 1, 2))
    w1 = cmp_w1.reshape(depth, 2, blk, 2, hh, hidden).transpose(0, 1, 4, 3, 2, 5)
    w1_t = _blockdiag2(w1).reshape(depth, 2, hh, 4 * blk, 2 * hidden)
    _, _, b1_e, w2_e, b2_e = cmpw
    return pos_t, w1_t.astype(BF16), b1_e, w2_e, b2_e


def _stack_heads(cols, lo):
    rows = []
    for cj in cols:
        rows.append(jnp.where(lo, cj, 0.0))
        rows.append(jnp.where(lo, 0.0, cj))
    return jnp.concatenate(rows, axis=0).astype(BF16)


def _block_ids(n, pb):
    return jnp.where(n >= pb, 2 * (n - pb) + 1, 2 * n)


def _nsa_prompt_body(q_ref, z_ref, gt_ref, tc_ref, ta_ref, tb_ref, ks_ref, vs_ref, kw_ref, vw_ref,
                     kc_ref, vc_ref, o_ref, *, tq, nsub, **kw):
    for sub in range(nsub):
        rows = slice(sub * tq, (sub + 1) * tq)
        o_ref[rows, :] = _nsa_tile(
            (pl.program_id(2) * nsub + sub) * tq, q_ref[rows, :], z_ref[rows, :], gt_ref[rows, :],
            tc_ref[rows, :], ta_ref[rows, :], tb_ref[rows, :], ks_ref, vs_ref, kw_ref, vw_ref,
            kc_ref, vc_ref, tq=tq, **kw).astype(o_ref.dtype)


def _nsa_tile(t0, q_in, z_in, g, tc, ta, tb, ks_ref, vs_ref, kw_ref, vw_ref, kc_ref, vc_ref,
              *, tq, tk, T, PB, blk, half, scale):
    nb = 2 * PB
    shift = blk.bit_length() - 1
    lo = lax.broadcasted_iota(jnp.int32, (tq, LANES), 1) < LANES // 2
    q = q_in * scale
    q0, q1 = q[:, 0:LANES], q[:, LANES:2 * LANES]
    qc = _stack_heads([q0, q1], lo)
    qr = _stack_heads([_rope(q0, tc, ta, tb, half), _rope(q1, tc, ta, tb, half)], lo)
    tpos = t0 + lax.broadcasted_iota(jnp.int32, (tq, 1), 0)
    cur = lax.shift_right_logical(tpos, shift)

    def tile4(x):
        return jnp.concatenate([x, x, x, x], axis=0)

    zrow = jnp.zeros((LANES - nb, LANES), BF16)
    kc = jnp.concatenate([kc_ref[0], kc_ref[1], zrow], axis=0)
    vc = jnp.concatenate([vc_ref[0], vc_ref[1], zrow], axis=0)
    bid = _block_ids(lax.broadcasted_iota(jnp.int32, (nb, tq), 0), PB)
    tpos_l = t0 + lax.broadcasted_iota(jnp.int32, (nb, tq), 1)
    cur_l = lax.shift_right_logical(tpos_l, shift)
    cmask = (bid + 1) * blk - 1 <= tpos_l

    def lanes4(x):
        return jnp.concatenate([x, x, x, x], axis=1)

    cm4 = lanes4(cmask.astype(F32)) > 0.5
    sc = lax.dot_general(kc, qc, NT_DIMS, preferred_element_type=F32)[0:nb]
    sc = jnp.where(cm4, sc, NEG)
    mc = jnp.max(sc, axis=0, keepdims=True)
    ec = jnp.where(cm4, jnp.exp(sc - mc), 0.0)
    pct = ec / jnp.maximum(jnp.sum(ec, axis=0, keepdims=True), 1e-30)
    imp = pct[:, 0:tq] + pct[:, tq:2 * tq] + pct[:, 2 * tq:3 * tq] + pct[:, 3 * tq:4 * tq]

    forced = (bid == 0) | (bid == cur_l) | (bid == cur_l - 1)
    valid = bid <= cur_l
    score = jnp.where(valid, jnp.where(forced, BIG, imp), -BIG)
    def ranks():
        cnt = jnp.zeros((nb, tq), F32)
        for i in range(nb):
            other = score[i:i + 1, :]
            bi = 2 * (i % PB) + i // PB
            beats = (other > score) | ((other == score) & (bid > bi))
            cnt = cnt + beats.astype(F32)
        return cnt

    cnt = lax.cond(t0 + tq > TOP_N * blk, ranks, lambda: jnp.zeros((nb, tq), F32))
    sel_t = ((cnt < TOP_N) & valid).astype(F32)

    zpad = jnp.zeros((LANES - nb, tq), F32)
    pc = jnp.concatenate(
        [jnp.transpose(jnp.concatenate([pct[:, hh * tq:(hh + 1) * tq], zpad], axis=0))
         for hh in range(4)], axis=0)
    oc = jnp.dot(pc.astype(BF16), vc, preferred_element_type=F32)
    sel = jnp.transpose(jnp.concatenate([sel_t, zpad], axis=0))

    sel = sel.astype(BF16)
    row_l = lax.broadcasted_iota(jnp.int32, (LANES, tk), 0)
    bid_col = jnp.where(row_l < nb, _block_ids(row_l, PB), -1)
    key_l = lax.broadcasted_iota(jnp.int32, (LANES, tk), 1)
    key_q = lax.broadcasted_iota(jnp.int32, (tq, tk), 1)

    def sel_chunk(ci, carry):
        m, l, acc = carry
        k0 = pl.multiple_of(ci * tk, tk)
        s = lax.dot_general(qr, ks_ref[pl.ds(k0, tk), :], NT_DIMS, preferred_element_type=F32)
        expand = (bid_col == lax.shift_right_logical(key_l + k0, shift)).astype(BF16)
        bm = jnp.dot(sel, expand, preferred_element_type=F32)
        ok = (bm > 0.5) & (key_q + k0 <= tpos)
        s = s + tile4(jnp.where(ok, 0.0, NEG))
        m_new = jnp.maximum(m, jnp.max(s, axis=-1, keepdims=True))
        alpha = jnp.exp(m - m_new)
        p = jnp.exp(s - m_new)
        l = alpha * l + jnp.sum(p, axis=-1, keepdims=True)
        acc = alpha * acc + jnp.dot(p.astype(BF16), vs_ref[pl.ds(k0, tk), :],
                                    preferred_element_type=F32)
        return m_new, l, acc

    nchunks = (t0 + tq + tk - 1) // tk
    init = (jnp.full((4 * tq, 1), NEG, F32), jnp.zeros((4 * tq, 1), F32),
            jnp.zeros((4 * tq, LANES), F32))
    _, ls, accs = lax.fori_loop(0, nchunks, sel_chunk, init)
    osel = accs / ls

    ww = min(WINDOW + tq, T)
    ws = pl.multiple_of(jnp.clip(t0 - WINDOW, 0, T - ww), tq)
    sw = lax.dot_general(qr, kw_ref[pl.ds(ws, ww), :], NT_DIMS, preferred_element_type=F32)
    dl = tpos - (ws + lax.broadcasted_iota(jnp.int32, (tq, ww), 1))
    sw = sw + tile4(jnp.where((dl >= 0) & (dl < WINDOW), 0.0, NEG))
    pw = jnp.exp(sw - jnp.max(sw, axis=-1, keepdims=True))
    ow = jnp.dot(pw.astype(BF16), vw_ref[pl.ds(ws, ww), :], preferred_element_type=F32)
    ow = ow / jnp.sum(pw, axis=-1, keepdims=True)

    def gcol(br):
        return jnp.concatenate([g[:, 3 * hh + br:3 * hh + br + 1] for hh in range(4)], axis=0)

    o = gcol(0) * oc + gcol(1) * osel + gcol(2) * ow
    p0 = jnp.where(lo, o[0:tq], o[tq:2 * tq])
    p1 = jnp.where(lo, o[2 * tq:3 * tq], o[3 * tq:4 * tq])
    return jnp.concatenate([p0, p1], axis=1) * _silu(z_in)


def _nsa_prompt(proj, gates, tabs, ks, vs, kw, vw, cmp_e, *, B, T, n_kv, head_dim, q_col, z_col,
                blk, tq, tk):
    nsub = max(n for n in (1, 2, 4) if T % (n * tq) == 0)
    ts = nsub * tq
    nq = T // ts
    gw = 2 * LANES
    PB = T // (2 * blk)
    assert tq == LANES and 2 * PB <= LANES and tk % tq == 0
    row = lambda b, h, i: b * nq + i
    tab = pl.BlockSpec((ts, LANES), lambda b, h, i: (i, 0))
    kvs = pl.BlockSpec((None, None, T, LANES), lambda b, h, i: (b, h, 0, 0))
    return pl.pallas_call(
        functools.partial(_nsa_prompt_body, tq=tq, nsub=nsub, tk=tk, T=T, PB=PB, blk=blk,
                          half=head_dim // 8, scale=head_dim ** -0.5),
        grid=(B, n_kv, nq),
        in_specs=[
            pl.BlockSpec((ts, gw), lambda b, h, i: (row(b, h, i), q_col // gw + h)),
            pl.BlockSpec((ts, gw), lambda b, h, i: (row(b, h, i), z_col // gw + h)),
            pl.BlockSpec((None, None, ts, LANES), lambda b, h, i: (b, h, i, 0)),
            tab, tab, tab,
            kvs, kvs, kvs, kvs,
            pl.BlockSpec((None, None, 2, PB, LANES), lambda b, h, i: (0, h, 0, b, 0)),
            pl.BlockSpec((None, None, 2, PB, LANES), lambda b, h, i: (1, h, 0, b, 0)),
        ],
        out_specs=pl.BlockSpec((ts, gw), lambda b, h, i: (row(b, h, i), h)),
        out_shape=jax.ShapeDtypeStruct((B * T, n_kv * gw), BF16),
        compiler_params=_cparams(("parallel", "parallel", "arbitrary")),
        name="nsa_prompt",
    )(proj, proj, gates, *tabs, ks, vs, kw, vw, cmp_e, cmp_e)


def _outproj_body(x_ref, a_ref, b_ref, wa_ref, wb_ref, g_ref, o_ref):
    acc = jnp.dot(a_ref[...].astype(BF16), wa_ref[...], preferred_element_type=F32)
    acc = acc + jnp.dot(b_ref[...].astype(BF16), wb_ref[...], preferred_element_type=F32)
    ms = jnp.mean(acc * acc, axis=-1, keepdims=True)
    o_ref[...] = x_ref[...] + (acc * lax.rsqrt(ms + EPS)) * g_ref[...]


def _outproj(x, a, b, w_out, g, l, *, tm):
    M, D = x.shape
    ka, kb = a.shape[1], b.shape[1]
    return pl.pallas_call(
        _outproj_body,
        grid=(M // tm,),
        in_specs=[
            pl.BlockSpec((tm, D), lambda i: (i, 0)),
            pl.BlockSpec((tm, ka), lambda i: (i, 0)),
            pl.BlockSpec((tm, kb), lambda i: (i, 0)),
            pl.BlockSpec((None, ka, D), lambda i: (l, 0, 0)),
            pl.BlockSpec((None, kb, D), lambda i: (l, ka // kb, 0)),
            pl.BlockSpec((None, 1, D), lambda i: (l, 0, 0)),
        ],
        out_specs=pl.BlockSpec((tm, D), lambda i: (i, 0)),
        out_shape=jax.ShapeDtypeStruct((M, D), F32),
        compiler_params=_cparams(("parallel",)),
        name="outproj",
    )(x, a, b, w_out, w_out, g)


def _sample_prep_body(p_ref, st_ref, w_ref, cb_ref, lg_ref, lb_ref, gb_ref,
                      tc_ref, ta_ref, tb_ref,
                      y_ref, cn_ref, kvn_ref, win_ref, qc_ref, qr_ref, gt_ref, zs_ref,
                      *, C, HD, kvw, half, scale):
    nb = p_ref.shape[0]
    taps = w_ref.shape[0]
    tc, ta, tb = tc_ref[...], ta_ref[...], tb_ref[...]
    glu = p_ref[:, 0:C] * jax.nn.sigmoid(p_ref[:, C:2 * C])
    c = w_ref[taps - 1:taps, :] * glu + cb_ref[...]
    for k in range(taps - 1):
        c = c + w_ref[k:k + 1, :] * st_ref[k]
    cn_ref[0:taps - 2] = st_ref[1:taps - 1]
    cn_ref[taps - 2] = glu
    mu = jnp.mean(c, axis=-1, keepdims=True)
    d = c - mu
    var = jnp.mean(d * d, axis=-1, keepdims=True)
    yl = (d * lax.rsqrt(var + EPS)) * lg_ref[...] + lb_ref[...]
    y_ref[...] = _silu(yl) * _silu(p_ref[:, 2 * C:3 * C])

    q_col = 3 * C
    z_col = q_col + HD
    kv_col = z_col + HD
    for j in range(HD // LANES):
        qj = p_ref[:, q_col + j * LANES:q_col + (j + 1) * LANES] * scale
        qc_ref[:, j * LANES:(j + 1) * LANES] = qj
        qr_ref[:, j * LANES:(j + 1) * LANES] = _rope(qj, tc, ta, tb, half)
    zs_ref[...] = _silu(p_ref[:, z_col:z_col + HD])
    kvn_ref[:, 0:2 * kvw] = p_ref[:, kv_col:kv_col + 2 * kvw]
    kvn_ref[:, 3 * kvw:4 * kvw] = p_ref[:, kv_col + 3 * kvw:kv_col + 4 * kvw]
    win_ref[:, kvw:2 * kvw] = p_ref[:, kv_col + 5 * kvw:kv_col + 6 * kvw]
    for j in range(kvw // LANES):
        s0 = kv_col + 2 * kvw + j * LANES
        kvn_ref[:, 2 * kvw + j * LANES:2 * kvw + (j + 1) * LANES] = _rope(
            p_ref[:, s0:s0 + LANES], tc, ta, tb, half)
        w0 = kv_col + 4 * kvw + j * LANES
        win_ref[:, j * LANES:(j + 1) * LANES] = _rope(p_ref[:, w0:w0 + LANES], tc, ta, tb, half)
    gl_col = kv_col + 6 * kvw
    gt_ref[...] = jax.nn.sigmoid(p_ref[:, gl_col:gl_col + LANES] + gb_ref[...])


def _sample_prep(proj, state_conv, conv_w, conv_b, ln_g, ln_b, gate_b, tabs, l,
                 *, C, HD, kvw, head_dim):
    nb = proj.shape[0]
    taps = conv_w.shape[1]
    whole = lambda shape: pl.BlockSpec(shape, lambda i: (0,) * len(shape))
    vecc = pl.BlockSpec((None, 1, C), lambda i: (l, 0, 0))
    tab = whole((1, LANES))
    shapes = [
        ((nb, C), F32), ((taps - 1, nb, C), F32), ((nb, 4 * kvw), F32), ((nb, 2 * kvw), F32),
        ((nb, HD), F32), ((nb, HD), F32), ((nb, LANES), F32), ((nb, HD), F32),
    ]
    return pl.pallas_call(
        functools.partial(_sample_prep_body, C=C, HD=HD, kvw=kvw, half=head_dim // 8,
                          scale=head_dim ** -0.5),
        grid=(1,),
        in_specs=[
            whole(proj.shape),
            pl.BlockSpec((None, taps - 1, nb, C), lambda i: (l, 0, 0, 0)),
            pl.BlockSpec((None, taps, C), lambda i: (l, 0, 0)),
            vecc, vecc, vecc,
            pl.BlockSpec((None, 1, LANES), lambda i: (l, 0, 0)),
            tab, tab, tab,
        ],
        out_specs=[whole(s) for s, _ in shapes],
        out_shape=[jax.ShapeDtypeStruct(s, d) for s, d in shapes],
        compiler_params=_cparams(("arbitrary",)),
        name="sample_prep",
    )(proj, state_conv, conv_w, conv_b, ln_g, ln_b, gate_b, *tabs)


def _sample_cmp_body(qc_ref, kc_ref, vc_ref, oc_ref, idx_ref, *, P, n_kv):
    b = pl.program_id(0)
    nb = 2 * P
    qrow = qc_ref[pl.ds(b, 1), :]
    lo = lax.broadcasted_iota(jnp.int32, (1, LANES), 1) < LANES // 2
    zero4 = jnp.zeros((4, LANES), F32)
    imps = []
    for h in range(n_kv):
        c0 = qrow[:, 2 * h * LANES:(2 * h + 1) * LANES]
        c1 = qrow[:, (2 * h + 1) * LANES:(2 * h + 2) * LANES]
        qs = jnp.concatenate([jnp.where(lo, c0, 0.0), jnp.where(lo, 0.0, c0),
                              jnp.where(lo, c1, 0.0), jnp.where(lo, 0.0, c1), zero4],
                             axis=0).astype(BF16)
        kc = jnp.concatenate([kc_ref[h, 0], kc_ref[h, 1]], axis=0)
        vc = jnp.concatenate([vc_ref[h, 0], vc_ref[h, 1]], axis=0)
        s = lax.dot_general(qs, kc, NT_DIMS, preferred_element_type=F32)
        e = jnp.exp(s - jnp.max(s, axis=-1, keepdims=True))
        pc = e / jnp.maximum(jnp.sum(e, axis=-1, keepdims=True), 1e-30)
        oc_ref[h] = jnp.dot(pc.astype(BF16), vc, preferred_element_type=F32)
        imps.append(pc[0:1] + pc[1:2] + pc[2:3] + pc[3:4])
    imp = jnp.concatenate(imps, axis=0)
    bid = _block_ids(lax.broadcasted_iota(jnp.int32, (n_kv, nb), 1), P)
    forced = (bid == 0) | (bid == nb - 1)
    score = jnp.where(forced, BIG, imp)
    cnt = (score < BIG).astype(F32)
    for i in range(nb):
        col = score[:, i:i + 1]
        beats = (col > score) | ((col == score) & (bid > 2 * (i % P) + i // P))
        cnt = cnt + beats.astype(F32)
    lane = lax.broadcasted_iota(jnp.int32, (n_kv, LANES), 1)
    res = jnp.full((n_kv, LANES), -1.0, F32)
    bid1 = (bid + 1).astype(F32)
    for k in range(TOP_N):
        v = jnp.sum(jnp.where(cnt == k, bid1, 0.0), axis=-1, keepdims=True) - 1.0
        res = jnp.where(lane == k, v, res)
    idx_ref[...] = res.astype(jnp.int32)


def _sample_cmp(qc, cmp_e, *, DB, P, n_kv):
    ce = lambda c: pl.BlockSpec((None, n_kv, 2, P, LANES), lambda b: (c, 0, 0, b, 0))
    return pl.pallas_call(
        functools.partial(_sample_cmp_body, P=P, n_kv=n_kv),
        grid=(DB,),
        in_specs=[pl.BlockSpec(qc.shape, lambda b: (0, 0)), ce(0), ce(1)],
        out_specs=[pl.BlockSpec((None, n_kv, 8, LANES), lambda b: (b, 0, 0, 0)),
                   pl.BlockSpec((None, n_kv, LANES), lambda b: (b, 0, 0))],
        out_shape=[jax.ShapeDtypeStruct((DB, n_kv, 8, LANES), F32),
                   jax.ShapeDtypeStruct((DB, n_kv, LANES), jnp.int32)],
        compiler_params=_cparams(("parallel",)),
        name="sample_cmp",
    )(qc, cmp_e, cmp_e)


def _sample_attn_body(idx_ref, pages_ref, qr_ref, gt_ref, zs_ref, oc_ref, kvn_ref, wr_ref, cw_ref,
                      cache_ref, o_ref, wn_ref, selbuf, sem, *, P, n_kv, hd, kvw, nwin):
    b = pl.program_id(0)

    def page_copy(pg, h, slot):
        return pltpu.make_async_copy(cache_ref.at[pg, pl.ds(2, 2), h], selbuf.at[slot], sem.at[0])

    for h in range(n_kv):
        for k in range(TOP_N):
            ix = jnp.maximum(idx_ref[(b * n_kv + h) * LANES + k], 0)
            page_copy(pages_ref[b * P + lax.shift_right_logical(ix, 1)], h, h * TOP_N + k).start()

    newrow = wr_ref[pl.ds(b, 1), :]
    eye = (lax.broadcasted_iota(jnp.int32, (hd, hd), 0)
           == lax.broadcasted_iota(jnp.int32, (hd, hd), 1))
    last = lax.broadcasted_iota(jnp.int32, (hd, nwin), 1) == nwin - 1
    for c in range(2):
        for h in range(n_kv):
            v = newrow[:, c * kvw + h * hd:c * kvw + (h + 1) * hd]
            col = jnp.sum(jnp.where(eye, v, 0.0), axis=1, keepdims=True)
            wn_ref[c, h] = jnp.where(last, col, pltpu.roll(cw_ref[c, h], nwin - 1, 1))

    for s in range(n_kv * TOP_N):
        page_copy(0, 0, s).wait()

    qrow = qr_ref[pl.ds(b, 1), :]
    kvrow = kvn_ref[pl.ds(b, 1), :]
    g = gt_ref[pl.ds(b, 1), :]
    zero4 = jnp.zeros((4, hd), F32)
    zero41 = jnp.zeros((4, 1), F32)
    page = selbuf.shape[3]
    rows = page // 2
    nk = TOP_N * page
    key = lax.broadcasted_iota(jnp.int32, (1, nk), 1)
    slot_of_key = key // page
    half_of_key = (key // rows) & 1
    pieces = []
    for h in range(n_kv):
        qh = jnp.concatenate([qrow[:, (4 * h + gg) * hd:(4 * h + gg + 1) * hd] for gg in range(4)]
                             + [zero4], axis=0)
        qb = qh.astype(BF16)
        ks_new = kvrow[:, 2 * kvw + h * hd:2 * kvw + (h + 1) * hd]
        vs_new = kvrow[:, 3 * kvw + h * hd:3 * kvw + (h + 1) * hd]
        kt = jnp.concatenate([selbuf[h * TOP_N + k, 0] for k in range(TOP_N)], axis=1).astype(BF16)
        vt = jnp.concatenate([selbuf[h * TOP_N + k, 1] for k in range(TOP_N)], axis=1).astype(BF16)
        want = jnp.full((1, nk), -1, jnp.int32)
        for k in range(TOP_N):
            ix = idx_ref[(b * n_kv + h) * LANES + k]
            want = jnp.where(slot_of_key == k, jnp.where(ix >= 0, ix & 1, -1), want)
        s_all = jnp.dot(qb, kt, preferred_element_type=F32)
        s_all = jnp.where(half_of_key == want, s_all, NEG)
        s_new = jnp.sum(qb.astype(F32) * ks_new.astype(BF16).astype(F32), axis=-1, keepdims=True)
        m = jnp.maximum(jnp.max(s_all, axis=-1, keepdims=True), s_new)
        e = jnp.exp(s_all - m)
        e_new = jnp.exp(s_new - m)
        lsum = jnp.sum(e, axis=-1, keepdims=True) + e_new
        acc = e_new.astype(BF16).astype(F32) * vs_new.astype(BF16).astype(F32)
        acc = acc + lax.dot_general(e.astype(BF16), vt, NT_DIMS, preferred_element_type=F32)
        osel = acc / lsum

        sw = jnp.dot(qb, wn_ref[0, h].astype(BF16), preferred_element_type=F32)
        pw = jnp.exp(sw - jnp.max(sw, axis=-1, keepdims=True))
        ow = lax.dot_general(pw.astype(BF16), wn_ref[1, h].astype(BF16), NT_DIMS,
                             preferred_element_type=F32)
        ow = ow / jnp.sum(pw, axis=-1, keepdims=True)

        def gcol(br):
            return jnp.concatenate(
                [g[:, (4 * h + gg) * 3 + br:(4 * h + gg) * 3 + br + 1] for gg in range(4)]
                + [zero41], axis=0)

        o = gcol(0) * oc_ref[h][:, 0:hd] + gcol(1) * osel + gcol(2) * ow
        for gg in range(4):
            pieces.append(o[gg:gg + 1, :])
    o_ref[pl.ds(b, 1), :] = jnp.concatenate(pieces, axis=1) * zs_ref[pl.ds(b, 1), :]


def _sample_attn(idx, pages, qr, gates, zs, oc, kvn, winrow, cache_win_t, cache_t, l,
                 *, DB, P, n_kv, head_dim):
    kvw = n_kv * head_dim
    nwin = cache_win_t.shape[-1]
    page = cache_t.shape[-1]
    win_blk = (None, 2, n_kv, head_dim, nwin)
    whole = lambda a: pl.BlockSpec(a.shape, lambda b, i, p: (0,) * a.ndim)
    grid_spec = pltpu.PrefetchScalarGridSpec(
        num_scalar_prefetch=2,
        grid=(DB,),
        in_specs=[
            whole(qr), whole(gates), whole(zs),
            pl.BlockSpec((None, n_kv, 8, LANES), lambda b, i, p: (b, 0, 0, 0)),
            whole(kvn), whole(winrow),
            pl.BlockSpec(win_blk, lambda b, i, p: (l * DB + b, 0, 0, 0, 0)),
            pl.BlockSpec(memory_space=pl.ANY),
        ],
        out_specs=[
            pl.BlockSpec(qr.shape, lambda b, i, p: (0, 0)),
            pl.BlockSpec(win_blk, lambda b, i, p: (b, 0, 0, 0, 0)),
        ],
        scratch_shapes=[
            pltpu.VMEM((n_kv * TOP_N, 2, head_dim, page), F32),
            pltpu.SemaphoreType.DMA((1,)),
        ],
    )
    return pl.pallas_call(
        functools.partial(_sample_attn_body, P=P, n_kv=n_kv, hd=head_dim, kvw=kvw, nwin=nwin),
        grid_spec=grid_spec,
        out_shape=[jax.ShapeDtypeStruct(qr.shape, F32),
                   jax.ShapeDtypeStruct((DB, 2, n_kv, head_dim, nwin), F32)],
        compiler_params=_cparams(("arbitrary",)),
        name="sample_attn",
    )(idx, pages, qr, gates, zs, oc, kvn, winrow, cache_win_t, cache_t)


def kernel(x_prompt, x_sample, cache_kv_pages, cache_win, state_conv, page_table, w_in, w_out,
           norm_pre, norm_post, conv_w, conv_b, conv_ln_g, conv_ln_b, cmp_pos, cmp_w1, cmp_b1,
           cmp_w2, cmp_b2, gate_b):
    B, S, D = x_prompt.shape
    DB, dec_seq, _ = x_sample.shape
    depth, n_pool, page, _, n_kv, head_dim = cache_kv_pages.shape
    n_pages = page_table.shape[1]
    past_len = n_pages * page
    C = conv_w.shape[-1]
    taps = conv_w.shape[1]
    n_heads = gate_b.shape[-1] // 3
    HD = n_heads * head_dim
    kvw = n_kv * head_dim
    blk = cmp_pos.shape[2]
    n_main = 3 * C + 2 * HD + 6 * kvw
    nwin = cache_win.shape[2]
    gpg = 3 * (n_heads // n_kv)
    assert dec_seq == 1 and page == 2 * blk and past_len % blk == 0
    assert w_in.shape[-1] == n_main + 3 * n_heads and n_heads == 4 * n_kv and head_dim * 2 == LANES
    assert nwin == WINDOW and S >= WINDOW and past_len // blk >= TOP_N and S % page == 0

    w_out_b = w_out.astype(BF16)
    gate_bp = jnp.pad(gate_b, ((0, 0), (0, LANES - 3 * n_heads))).reshape(depth, 1, LANES)
    cmpw = _compress_weights(cmp_pos, cmp_w1, cmp_b1, cmp_w2, cmp_b2)
    cmpw_c = _compress_cache_weights(cmp_pos, cmp_w1, cmpw)
    vec3 = lambda v: v.reshape(depth, 1, v.shape[-1])
    norm_pre3, norm_post3 = vec3(norm_pre), vec3(norm_post)
    conv_b3, ln_g3, ln_b3 = vec3(conv_b), vec3(conv_ln_g), vec3(conv_ln_b)

    tabs_p = _rope_tables(jnp.arange(S, dtype=jnp.int32), head_dim)
    tabs_s = _rope_tables(jnp.full((1,), past_len, jnp.int32), head_dim)

    cache_t = cache_kv_pages.transpose(0, 1, 3, 4, 5, 2).reshape(depth * n_pool, 4, n_kv, head_dim, page)
    cache_t6 = cache_t.reshape(depth * n_pool, 4, n_kv, 2, head_dim // 2, page)
    cache_win_t = cache_win.transpose(0, 1, 3, 4, 5, 2).reshape(depth * DB, 2, n_kv, head_dim, nwin)
    state_t = state_conv.transpose(0, 2, 1, 3)
    w_in_t = w_in.transpose(0, 2, 1).astype(BF16)
    pages_prompt = jnp.arange(B * S // page, dtype=jnp.int32)
    conv_zero = jnp.zeros((B, taps - 1, C), F32)

    xp = x_prompt.reshape(B * S, D)
    xs = x_sample.reshape(DB, D)
    tm_in = 1024 if (B * S) % 1024 == 0 else 256
    tm_out = 512 if (B * S) % 512 == 0 else 256
    outs = [[] for _ in range(6)]
    for l in range(depth):
        proj = _inproj(xp, norm_pre3, w_in_t, l, tm=tm_in, tn=1024)
        yconv, conv_new = _conv_prompt(proj, conv_zero, conv_w, conv_b3, ln_g3, ln_b3, l,
                                       B=B, T=S, C=C, tt=512)
        kvn, winr, ks, vs, kw, vw, gates = _assemble_prompt(
            proj, gate_bp, tabs_p, l, B=B, T=S, n_kv=n_kv, head_dim=head_dim,
            kv_col=3 * C + 2 * HD, gl_col=n_main, tt=512, gpg=gpg, nwin=nwin)
        cmp_e = _compress(pages_prompt, proj.reshape(B * S // page, 2, 2, blk // 2, proj.shape[1]),
                          cmpw, l, G=1, P=B * S // page, col0=3 * C + 2 * HD, n_kv=n_kv,
                          head_dim=head_dim)
        attn = _nsa_prompt(proj, gates, tabs_p, ks, vs, kw, vw, cmp_e, B=B, T=S, n_kv=n_kv,
                           head_dim=head_dim, q_col=3 * C, z_col=3 * C + HD, blk=blk, tq=128, tk=512)
        xp = _outproj(xp, yconv, attn, w_out_b, norm_post3, l, tm=tm_out)
        outs[0].append(kvn.transpose(0, 4, 1, 2, 3))
        outs[2].append(winr.transpose(0, 4, 1, 2, 3))
        outs[4].append(conv_new)

        proj_s = _inproj(xs, norm_pre3, w_in_t, l, tm=DB, tn=1024)
        (yconv_s, conv_new_s, kvn_s, winrow_s, qc_s, qr_s, gates_s, zs_s) = _sample_prep(
            proj_s, state_t, conv_w, conv_b3, ln_g3, ln_b3, gate_bp, tabs_s, l,
            C=C, HD=HD, kvw=kvw, head_dim=head_dim)
        pages_s = (page_table + l * n_pool).reshape(-1).astype(jnp.int32)
        cmp_s = _compress_cache(pages_s, cache_t6, cmpw_c, l, G=DB, P=n_pages)
        oc_s, idx_s = _sample_cmp(qc_s, cmp_s, DB=DB, P=n_pages, n_kv=n_kv)
        attn_s, win_new_s = _sample_attn(idx_s.reshape(-1), pages_s, qr_s, gates_s, zs_s, oc_s, kvn_s,
                                         winrow_s, cache_win_t, cache_t, l,
                                         DB=DB, P=n_pages, n_kv=n_kv, head_dim=head_dim)
        xs = _outproj(xs, yconv_s, attn_s, w_out_b, norm_post3, l, tm=DB)
        outs[1].append(kvn_s.reshape(DB, 1, 4, n_kv, head_dim))
        outs[3].append(win_new_s.transpose(0, 4, 1, 2, 3))
        outs[5].append(conv_new_s.transpose(1, 0, 2))

    return (xp.reshape(B, S, D), xs.reshape(DB, 1, D), jnp.stack(outs[0]), jnp.stack(outs[1]),
            jnp.stack(outs[2]), jnp.stack(outs[3]), jnp.stack(outs[4]), jnp.stack(outs[5]))
```
